```python
import math
import jax, jax.numpy as jnp
from jax import lax
import numpy as np


D_MODEL = 1024
BATCH = 8
SEQ = 2048
DEPTH = 2
DEC_BATCH = 128
DEC_SEQ = 1
PAST_LEN = 16384
PAGE_SIZE = 128

D_PLE = 256
D_A = D_MODEL // 2
D_B = D_MODEL // 2
D_C = D_MODEL // 2
CONV_A_WIDTH = 31
CHUNK = 128
N_GROUPS_B = 4
GW_B = D_B // N_GROUPS_B
SSM_GROUP = 16
N_GROUPS_C = D_C // SSM_GROUP
SSM_STATE = 64
N_BRANCH = 3
D_FF = 2816
FFN_CONV_WIDTH = 3
IN_COLS = 2 * D_A + 2 * D_B + D_C + N_BRANCH * D_MODEL
EPS = 1e-6

kernel_name = 'hybrid_conv_gmlp_s5_decoder'


def rmsnorm(x, g):
    xf = x.astype(jnp.float32)
    y = xf * lax.rsqrt(jnp.mean(xf * xf, axis=-1, keepdims=True) + EPS)
    return (y * g.astype(jnp.float32)).astype(x.dtype)


def layernorm(x, g, b):
    xf = x.astype(jnp.float32)
    mu = jnp.mean(xf, axis=-1, keepdims=True)
    xc = xf - mu
    var = jnp.mean(xc * xc, axis=-1, keepdims=True)
    y = xc * lax.rsqrt(var + EPS) * g.astype(jnp.float32) + b.astype(jnp.float32)
    return y.astype(x.dtype)


def causal_dwconv(x, buf, w, b):
    xp = jnp.concatenate([buf.astype(x.dtype), x], axis=1)
    y = lax.conv_general_dilated(xp, w[:, None, :].astype(x.dtype), (1,), 'VALID',
                                 dimension_numbers=('NWC', 'WIO', 'NWC'),
                                 feature_group_count=x.shape[-1])
    k1 = w.shape[0] - 1
    return y + b.astype(x.dtype), xp[:, xp.shape[1] - k1:]


def chunk_spatial_mix(v, w_s, b_s):
    bt, t, c = v.shape
    nc = -(-t // CHUNK)
    vp = jnp.pad(v, ((0, 0), (0, nc * CHUNK - t), (0, 0)))
    vp = vp.reshape(bt, nc, CHUNK, N_GROUPS_B, GW_B)
    mask = jnp.tril(jnp.ones((CHUNK, CHUNK), dtype=bool))
    w = jnp.where(mask, w_s, 0).astype(v.dtype)
    out = jnp.einsum('gij,bcjgd->bcigd', w, vp) + b_s.T[:, :, None].astype(v.dtype)
    return out.reshape(bt, nc * CHUNK, c)[:, :t]


def s5_scan(u, s0_re, s0_im, a_re, a_im, log_dt, b_re, b_im, c_re, c_im, d):
    f32 = jnp.float32
    bt, t, _ = u.shape
    uf = u.astype(f32).reshape(bt, t, N_GROUPS_C, SSM_GROUP)
    A = lax.complex(a_re.astype(f32), a_im.astype(f32))
    dt = jnp.exp(log_dt.astype(f32))[:, None]
    a_bar = jnp.exp(A * dt)
    B = lax.complex(b_re.astype(f32), b_im.astype(f32))
    b_bar = ((a_bar - 1.0) / A)[:, :, None] * B
    C = lax.complex(c_re.astype(f32), c_im.astype(f32))
    bu = jnp.einsum('gnh,btgh->btgn', b_bar, uf.astype(jnp.complex64))
    s0 = lax.complex(s0_re.astype(f32), s0_im.astype(f32))
    bu = bu.at[:, 0].add(a_bar * s0)
    a_seq = jnp.broadcast_to(a_bar, bu.shape)

    def combine(l, r):
        return (r[0] * l[0], r[0] * l[1] + r[1])

    _, s = lax.associative_scan(combine, (a_seq, bu), axis=1)
    y = jnp.real(jnp.einsum('ghn,btgn->btgh', C, s)) + d.astype(f32).reshape(N_GROUPS_C, SSM_GROUP) * uf
    s_last = s[:, -1]
    return y.reshape(bt, t, D_C).astype(u.dtype), jnp.real(s_last), jnp.imag(s_last)


def mixer_layer(h, buf_a, s_re, s_im, prm, i):
    z = h @ prm['w_in'][i]
    o1 = 2 * D_A
    o2 = o1 + 2 * D_B
    o3 = o2 + D_C
    za, zb, zc, zg = z[..., :o1], z[..., o1:o2], z[..., o2:o3], z[..., o3:]
    a = za[..., :D_A] * jax.nn.sigmoid(za[..., D_A:])
    a_conv, new_buf_a = causal_dwconv(a, buf_a, prm['conv_a_w'][i], prm['conv_a_b'][i])
    a_out = jax.nn.silu(layernorm(a_conv, prm['ln_a_g'][i], prm['ln_a_b'][i])) @ prm['w_a_out'][i]
    u, v = zb[..., :D_B], zb[..., D_B:]
    v = layernorm(v, prm['ln_b_g'][i], prm['ln_b_b'][i])
    vm = chunk_spatial_mix(v, prm['w_s'][i], prm['b_s'][i])
    b_out = (u * vm) @ prm['w_b_out'][i]
    t = v.shape[1]
    v_state = v[:, ((t - 1) // CHUNK) * CHUNK:]
    yc, new_re, new_im = s5_scan(zc, s_re, s_im, prm['ssm_a_re'][i], prm['ssm_a_im'][i],
                                 prm['ssm_log_dt'][i], prm['ssm_b_re'][i], prm['ssm_b_im'][i],
                                 prm['ssm_c_re'][i], prm['ssm_c_im'][i], prm['ssm_d'][i])
    gc = jax.nn.gelu(yc) @ prm['w_c_glu'][i]
    c_out = gc[..., :D_MODEL] * jax.nn.sigmoid(gc[..., D_MODEL:])
    g = jax.nn.sigmoid(zg)
    m = (g[..., :D_MODEL] * a_out + g[..., D_MODEL:2 * D_MODEL] * b_out
         + g[..., 2 * D_MODEL:] * c_out)
    return m @ prm['w_out'][i], new_buf_a, v_state, new_re, new_im


def conv_ffn(h, buf, w_up, cw, cb, w_down):
    up, nb = causal_dwconv(h @ w_up, buf, cw, cb)
    gate, val = up[..., :D_FF], up[..., D_FF:]
    return (jax.nn.gelu(gate) * val) @ w_down, nb


def trunk(x, p, buf_a, s_re, s_im, buf_f, prm):
    la, lv, lre, lim, lf = [], [], [], [], []
    for i in range(DEPTH):
        h = rmsnorm(x, prm['g_mix'][i])
        m, na, nv, nre, nim = mixer_layer(h, buf_a[i], s_re[i], s_im[i], prm, i)
        x = x + m
        f, nf = conv_ffn(rmsnorm(x, prm['g_ffn'][i]), buf_f[i], prm['w_up'][i],
                         prm['conv_f_w'][i], prm['conv_f_b'][i], prm['w_down'][i])
        x = x + f
        gate = jax.nn.sigmoid(rmsnorm(x, prm['g_ple'][i]) @ prm['w_pg'][i])
        x = x + gate * (p[i].astype(x.dtype) @ prm['w_pe'][i])
        la.append(na); lv.append(nv); lre.append(nre); lim.append(nim); lf.append(nf)
    y = rmsnorm(x, prm['g_final'])
    return (y, jnp.stack(la), jnp.stack(lv), jnp.stack(lre), jnp.stack(lim), jnp.stack(lf))


def setup_inputs(seed: int = 0) -> dict:
    key = jax.random.key(seed)
    ks = iter(jax.random.split(key, 64))
    f32 = jnp.float32

    def nrm(shape, scale=1.0):
        return jax.random.normal(next(ks), shape, f32) * scale

    def gain(shape):
        return 1.0 + nrm(shape, 0.01)

    L = DEPTH
    a_im_base = jnp.pi * jnp.arange(SSM_STATE, dtype=f32)
    return {
        'x_prompt': nrm((BATCH, SEQ, D_MODEL)),
        'x_sample': nrm((DEC_BATCH, DEC_SEQ, D_MODEL)),
        'state_conv_a': nrm((L, DEC_BATCH, CONV_A_WIDTH - 1, D_A), 0.5),
        'state_ssm_re': nrm((L, DEC_BATCH, N_GROUPS_C, SSM_STATE)),
        'state_ssm_im': nrm((L, DEC_BATCH, N_GROUPS_C, SSM_STATE)),
        'state_conv_ffn': nrm((L, DEC_BATCH, FFN_CONV_WIDTH - 1, 2 * D_FF), 0.5),
        'p_prompt': nrm((L, BATCH, SEQ, D_PLE)),
        'p_sample': nrm((L, DEC_BATCH, DEC_SEQ, D_PLE)),
        'g_mix': gain((L, D_MODEL)),
        'w_in': nrm((L, D_MODEL, IN_COLS), D_MODEL ** -0.5),
        'conv_a_w': nrm((L, CONV_A_WIDTH, D_A), CONV_A_WIDTH ** -0.5),
        'conv_a_b': nrm((L, D_A), 0.01),
        'ln_a_g': gain((L, D_A)),
        'ln_a_b': nrm((L, D_A), 0.01),
        'w_a_out': nrm((L, D_A, D_MODEL), D_A ** -0.5),
        'ln_b_g': gain((L, D_B)),
        'ln_b_b': nrm((L, D_B), 0.01),
        'w_s': nrm((L, N_GROUPS_B, CHUNK, CHUNK), CHUNK ** -0.5),
        'b_s': gain((L, N_GROUPS_B, CHUNK)),
        'w_b_out': nrm((L, D_B, D_MODEL), D_B ** -0.5),
        'ssm_a_re': -0.5 + nrm((L, N_GROUPS_C, SSM_STATE), 0.01),
        'ssm_a_im': a_im_base + nrm((L, N_GROUPS_C, SSM_STATE), 0.01),
        'ssm_log_dt': jax.random.uniform(next(ks), (L, N_GROUPS_C), f32,
                                         math.log(1e-3), math.log(1e-1)),
        'ssm_b_re': nrm((L, N_GROUPS_C, SSM_STATE, SSM_GROUP), (2.0 * SSM_GROUP) ** -0.5),
        'ssm_b_im': nrm((L, N_GROUPS_C, SSM_STATE, SSM_GROUP), (2.0 * SSM_GROUP) ** -0.5),
        'ssm_c_re': nrm((L, N_GROUPS_C, SSM_GROUP, SSM_STATE), (2.0 * SSM_STATE) ** -0.5),
        'ssm_c_im': nrm((L, N_GROUPS_C, SSM_GROUP, SSM_STATE), (2.0 * SSM_STATE) ** -0.5),
        'ssm_d': nrm((L, D_C)),
        'w_c_glu': nrm((L, D_C, 2 * D_MODEL), D_C ** -0.5),
        'w_out': nrm((L, D_MODEL, D_MODEL), D_MODEL ** -0.5),
        'g_ffn': gain((L, D_MODEL)),
        'w_up': nrm((L, D_MODEL, 2 * D_FF), D_MODEL ** -0.5),
        'conv_f_w': nrm((L, FFN_CONV_WIDTH, 2 * D_FF), FFN_CONV_WIDTH ** -0.5),
        'conv_f_b': nrm((L, 2 * D_FF), 0.01),
        'w_down': nrm((L, D_FF, D_MODEL), D_FF ** -0.5),
        'g_ple': gain((L, D_MODEL)),
        'w_pg': nrm((L, D_MODEL, D_MODEL), D_MODEL ** -0.5),
        'w_pe': nrm((L, D_PLE, D_MODEL), D_PLE ** -0.5),
        'g_final': gain((D_MODEL,)),
    }


def reference(x_prompt, x_sample, state_conv_a, state_ssm_re, state_ssm_im, state_conv_ffn,
              p_prompt, p_sample, g_mix, w_in, conv_a_w, conv_a_b, ln_a_g, ln_a_b, w_a_out,
              ln_b_g, ln_b_b, w_s, b_s, w_b_out, ssm_a_re, ssm_a_im, ssm_log_dt, ssm_b_re,
              ssm_b_im, ssm_c_re, ssm_c_im, ssm_d, w_c_glu, w_out, g_ffn, w_up, conv_f_w,
              conv_f_b, w_down, g_ple, w_pg, w_pe, g_final):
    prm = dict(g_mix=g_mix, w_in=w_in, conv_a_w=conv_a_w, conv_a_b=conv_a_b, ln_a_g=ln_a_g,
               ln_a_b=ln_a_b, w_a_out=w_a_out, ln_b_g=ln_b_g, ln_b_b=ln_b_b, w_s=w_s, b_s=b_s,
               w_b_out=w_b_out, ssm_a_re=ssm_a_re, ssm_a_im=ssm_a_im, ssm_log_dt=ssm_log_dt,
               ssm_b_re=ssm_b_re, ssm_b_im=ssm_b_im, ssm_c_re=ssm_c_re, ssm_c_im=ssm_c_im,
               ssm_d=ssm_d, w_c_glu=w_c_glu, w_out=w_out, g_ffn=g_ffn, w_up=w_up,
               conv_f_w=conv_f_w, conv_f_b=conv_f_b, w_down=w_down, g_ple=g_ple, w_pg=w_pg,
               w_pe=w_pe, g_final=g_final)
    bp = x_prompt.shape[0]
    dt_p = x_prompt.dtype
    zero_a = jnp.zeros((DEPTH, bp, CONV_A_WIDTH - 1, D_A), dt_p)
    zero_re = jnp.zeros((DEPTH, bp, N_GROUPS_C, SSM_STATE), jnp.float32)
    zero_f = jnp.zeros((DEPTH, bp, FFN_CONV_WIDTH - 1, 2 * D_FF), dt_p)
    y_prompt, conv_a_p, chunk_v_p, ssm_re_p, ssm_im_p, conv_ffn_p = trunk(
        x_prompt, p_prompt, zero_a, zero_re, zero_re, zero_f, prm)
    y_sample, conv_a_s, chunk_v_s, ssm_re_s, ssm_im_s, conv_ffn_s = trunk(
        x_sample, p_sample, state_conv_a, state_ssm_re, state_ssm_im, state_conv_ffn, prm)
    return (y_prompt, y_sample, conv_a_p, conv_a_s, chunk_v_p, chunk_v_s,
            ssm_re_p, ssm_im_p, ssm_re_s, ssm_im_s, conv_ffn_p, conv_ffn_s)
```

```python
import functools

import jax
import jax.numpy as jnp
from jax import lax
from jax.experimental import pallas as pl
from jax.experimental.pallas import tpu as pltpu

D_MODEL = 1024
D_PLE = 256
D_A = 512
D_B = 512
D_C = 512
CONV_A_WIDTH = 31
CHUNK = 128
N_GROUPS_B = 4
GW_B = D_B // N_GROUPS_B
SSM_GROUP = 16
N_GROUPS_C = D_C // SSM_GROUP
SSM_STATE = 64
D_FF = 2816
FFN_CONV_WIDTH = 3
EPS = 1e-6

N_STATE = N_GROUPS_C * SSM_STATE
SSM_SUB = 4
SUB_IN = D_C // SSM_SUB
SUB_ST = N_STATE // SSM_SUB
HIST_A = CONV_A_WIDTH - 1
HIST_F = FFN_CONV_WIDTH - 1
O_B = 2 * D_A
O_C = O_B + 2 * D_B
O_G = O_C + D_C
FF_CHUNKS = ((0, 1024), (1024, 1024), (2048, 768))
CONV_ROWS = 32
VMEM_LIMIT = 58 * 1024 * 1024

F32 = jnp.float32
BF16 = jnp.bfloat16


def _rms(x, g):
    ms = jnp.mean(x * x, axis=-1, keepdims=True)
    return x * lax.rsqrt(ms + EPS) * g


def _ln(x, g, b):
    mu = jnp.mean(x, axis=-1, keepdims=True)
    xc = x - mu
    var = jnp.mean(xc * xc, axis=-1, keepdims=True)
    return xc * lax.rsqrt(var + EPS) * g + b


def _dot(a, b):
    return jnp.dot(a, b, preferred_element_type=F32)


def _mixer_kernel(cfg, x_ref, hist_ref, s0re_ref, s0im_ref, gmix_ref, win_ref, caw_ref,
                  cab_ref, lag_ref, lab_ref, waout_ref, lbg_ref, lbb_ref, kron_ref, bsb_ref,
                  wbout_ref, are_ref, aim_ref, bblk_ref, cblk_ref, ssmd_ref, wcglu_ref,
                  wout_ref,
                  x1_ref, nhist_ref, cv_ref, nsre_ref, nsim_ref,
                  hist_s, vprev_s, sst_s):
    nb, tt, nt, npos = cfg
    rows = nb * tt
    hb = HIST_A * nb
    i = pl.program_id(0)

    @pl.when(i == 0)
    def _init():
        hist_s[...] = hist_ref[...]
        sst_s[0] = s0re_ref[...]
        sst_s[1] = s0im_ref[...]
        vprev_s[...] = jnp.zeros(vprev_s.shape, vprev_s.dtype)

    x = x_ref[...]
    h = _rms(x, gmix_ref[...]).astype(BF16)

    za = _dot(h, win_ref[:, 0:O_B])
    a = za[:, :D_A] * jax.nn.sigmoid(za[:, D_A:])
    cat = jnp.concatenate([hist_s[...], a], axis=0)
    new_hist = cat[rows:rows + hb]
    hist_s[...] = new_hist

    @pl.when(i == nt - 1)
    def _emit_hist():
        nhist_ref[...] = new_hist

    cab = cab_ref[...]
    conv_blocks = []
    rb = min(CONV_ROWS, rows)
    for r0 in range(0, rows, rb):
        acc = jnp.broadcast_to(cab, (rb, D_A))
        for k in range(CONV_A_WIDTH):
            acc = acc + caw_ref[k:k + 1, :] * cat[k * nb + r0:k * nb + r0 + rb]
        conv_blocks.append(acc)
    a_conv = jnp.concatenate(conv_blocks, axis=0) if len(conv_blocks) > 1 else conv_blocks[0]
    a_act = _ln(a_conv, lag_ref[...], lab_ref[...])
    a_act = (a_act * jax.nn.sigmoid(a_act)).astype(BF16)
    a_out = _dot(a_act, waout_ref[...])
    m = jax.nn.sigmoid(_dot(h, win_ref[:, O_G:O_G + D_MODEL])) * a_out

    zb = _dot(h, win_ref[:, O_B:O_C])
    u = zb[:, :D_B]
    v = _ln(zb[:, D_B:], lbg_ref[...], lbb_ref[...])
    cv_ref[...] = v
    v16 = v.astype(BF16)
    slots = [v16] if npos == 1 else [vprev_s[...].astype(BF16), v16]
    vm_parts = []
    for g in range(N_GROUPS_B):
        cs = slice(g * GW_B, (g + 1) * GW_B)
        part = None
        for s, vs in enumerate(slots):
            t = _dot(kron_ref[0, s, g], vs[:, cs])
            part = t if part is None else part + t
        vm_parts.append(part)
    vm = jnp.concatenate(vm_parts, axis=1) + bsb_ref[0]
    if npos > 1:
        vprev_s[...] = v
    b_out = _dot((u * vm).astype(BF16), wbout_ref[...])
    m = m + jax.nn.sigmoid(_dot(h, win_ref[:, O_G + D_MODEL:O_G + 2 * D_MODEL])) * b_out

    zc = _dot(h, win_ref[:, O_C:O_G])
    y_parts = []
    for q in range(SSM_SUB):
        uq = zc[:, q * SUB_IN:(q + 1) * SUB_IN]
        bu = _dot(uq.astype(BF16), bblk_ref[q])
        st = slice(q * SUB_ST, (q + 1) * SUB_ST)
        ar = jnp.broadcast_to(are_ref[:, st], (nb, SUB_ST))
        ai = jnp.broadcast_to(aim_ref[:, st], (nb, SUB_ST))
        sr = sst_s[0, :, st]
        si = sst_s[1, :, st]
        s_rows = []
        for t in range(tt):
            bt = bu[t * nb:(t + 1) * nb]
            nr = ar * sr - ai * si + bt[:, :SUB_ST]
            ni = ar * si + ai * sr + bt[:, SUB_ST:]
            sr, si = nr, ni
            s_rows.append(jnp.concatenate([nr, ni], axis=1))
        sst_s[0, :, st] = sr
        sst_s[1, :, st] = si
        s_all = jnp.concatenate(s_rows, axis=0) if tt > 1 else s_rows[0]
        yq = _dot(s_all.astype(BF16), cblk_ref[q]) + ssmd_ref[:, q * SUB_IN:(q + 1) * SUB_IN] * uq
        y_parts.append(jax.nn.gelu(yq).astype(BF16))
    yc = jnp.concatenate(y_parts, axis=1)

    @pl.when(i == nt - 1)
    def _emit_state():
        nsre_ref[...] = sst_s[0]
        nsim_ref[...] = sst_s[1]

    gc = _dot(yc, wcglu_ref[...])
    c_out = gc[:, :D_MODEL] * jax.nn.sigmoid(gc[:, D_MODEL:])
    m = m + jax.nn.sigmoid(_dot(h, win_ref[:, O_G + 2 * D_MODEL:O_G + 3 * D_MODEL])) * c_out

    x1_ref[...] = x + _dot(m.astype(BF16), wout_ref[...])


def _ffn_kernel(cfg, x_ref, p_ref, hist_ref, gffn_ref, wup_ref, cfw_ref, cfb_ref, wdown_ref,
                gple_ref, wpg_ref, wpe_ref, gfin_ref,
                xo_ref, nhist_ref,
                hist_s):
    nb, tt, nt, final = cfg
    rows = nb * tt
    hb = HIST_F * nb
    i = pl.program_id(0)

    @pl.when(i == 0)
    def _init():
        hist_s[...] = hist_ref[...]

    x = x_ref[...]
    h = _rms(x, gffn_ref[...]).astype(BF16)
    f = None
    for c0, cw in FF_CHUNKS:
        halves = []
        for off in (c0, D_FF + c0):
            up = _dot(h, wup_ref[:, off:off + cw])
            cat = jnp.concatenate([hist_s[:, off:off + cw], up], axis=0)
            new_hist = cat[rows:rows + hb]
            hist_s[:, off:off + cw] = new_hist

            @pl.when(i == nt - 1)
            def _emit(new_hist=new_hist, off=off, cw=cw):
                nhist_ref[:, off:off + cw] = new_hist

            y = cfb_ref[:, off:off + cw] + cfw_ref[0:1, off:off + cw] * cat[0:rows]
            y = y + cfw_ref[1:2, off:off + cw] * cat[nb:nb + rows]
            y = y + cfw_ref[2:3, off:off + cw] * up
            halves.append(y)
        act = (jax.nn.gelu(halves[0]) * halves[1]).astype(BF16)
        t = _dot(act, wdown_ref[c0:c0 + cw, :])
        f = t if f is None else f + t
    x2 = x + f
    h3 = _rms(x2, gple_ref[...]).astype(BF16)
    gate = jax.nn.sigmoid(_dot(h3, wpg_ref[...]))
    x3 = x2 + gate * _dot(p_ref[...], wpe_ref[...])
    if final:
        x3 = _rms(x3, gfin_ref[...])
    xo_ref[...] = x3


def _const_spec(shape):
    nd = len(shape)
    return pl.BlockSpec(shape, lambda i: (0,) * nd, pipeline_mode=pl.Buffered(1))


def _pos_spec(shape, npos):
    nd = len(shape)
    return pl.BlockSpec((1,) + tuple(shape[1:]), lambda i: (i % npos,) + (0,) * (nd - 1))


def _out_const_spec(shape):
    nd = len(shape)
    return pl.BlockSpec(shape, lambda i: (0,) * nd)


def _mixer_call(nb, tt, nt, x, hist0, s0re, s0im, w):
    rows = nb * tt
    t_total = tt * nt
    npos = w['kron'].shape[0]
    last_start = ((t_total - 1) // CHUNK) * CHUNK
    n_last = (t_total - last_start) // tt
    cfg = (nb, tt, nt, npos)
    consts = [hist0, s0re, s0im, w['g_mix'], w['w_in'], w['conv_a_w'], w['conv_a_b'],
              w['ln_a_g'], w['ln_a_b'], w['w_a_out'], w['ln_b_g'], w['ln_b_b'], w['kron'],
              w['bsb'], w['w_b_out'], w['a_re'], w['a_im'], w['bblk'], w['cblk'], w['ssm_d'],
              w['w_c_glu'], w['w_out']]
    in_specs = [pl.BlockSpec((rows, D_MODEL), lambda i: (i, 0))]
    in_specs += [_pos_spec(c.shape, npos) if c is w['kron'] or c is w['bsb']
                 else _const_spec(c.shape) for c in consts]
    out_shape = [
        jax.ShapeDtypeStruct((nt * rows, D_MODEL), F32),
        jax.ShapeDtypeStruct((HIST_A * nb, D_A), F32),
        jax.ShapeDtypeStruct((n_last * rows, D_B), F32),
        jax.ShapeDtypeStruct((nb, N_STATE), F32),
        jax.ShapeDtypeStruct((nb, N_STATE), F32),
    ]
    out_specs = [
        pl.BlockSpec((rows, D_MODEL), lambda i: (i, 0)),
        _out_const_spec((HIST_A * nb, D_A)),
        pl.BlockSpec((rows, D_B), lambda i: (jnp.maximum(i - (nt - n_last), 0), 0)),
        _out_const_spec((nb, N_STATE)),
        _out_const_spec((nb, N_STATE)),
    ]
    scratch = [
        pltpu.VMEM((HIST_A * nb, D_A), F32),
        pltpu.VMEM((rows, D_B), F32),
        pltpu.VMEM((2, nb, N_STATE), F32),
    ]
    return pl.pallas_call(
        functools.partial(_mixer_kernel, cfg),
        grid=(nt,),
        in_specs=in_specs,
        out_specs=out_specs,
        out_shape=out_shape,
        scratch_shapes=scratch,
        compiler_params=pltpu.CompilerParams(
            dimension_semantics=("arbitrary",), vmem_limit_bytes=VMEM_LIMIT),
        name=f"mixer_b{nb}_t{tt}",
    )(x, *consts)


def _ffn_call(nb, tt, nt, final, x, p, hist0, w, g_final):
    rows = nb * tt
    cfg = (nb, tt, nt, final)
    consts = [hist0, w['g_ffn'], w['w_up'], w['conv_f_w'], w['conv_f_b'], w['w_down'],
              w['g_ple'], w['w_pg'], w['w_pe'], g_final]
    in_specs = [pl.BlockSpec((rows, D_MODEL), lambda i: (i, 0)),
                pl.BlockSpec((rows, D_PLE), lambda i: (i, 0))]
    in_specs += [_const_spec(c.shape) for c in consts]
    out_shape = [
        jax.ShapeDtypeStruct((nt * rows, D_MODEL), F32),
        jax.ShapeDtypeStruct((HIST_F * nb, 2 * D_FF), F32),
    ]
    out_specs = [
        pl.BlockSpec((rows, D_MODEL), lambda i: (i, 0)),
        _out_const_spec((HIST_F * nb, 2 * D_FF)),
    ]
    return pl.pallas_call(
        functools.partial(_ffn_kernel, cfg),
        grid=(nt,),
        in_specs=in_specs,
        out_specs=out_specs,
        out_shape=out_shape,
        scratch_shapes=[pltpu.VMEM((HIST_F * nb, 2 * D_FF), F32)],
        compiler_params=pltpu.CompilerParams(
            dimension_semantics=("arbitrary",), vmem_limit_bytes=VMEM_LIMIT),
        name=f"ffn_b{nb}_t{tt}",
    )(x, p, *consts)


def _kron_weights(w_s, b_s, nb, tt, npos):
    mask = jnp.tril(jnp.ones((CHUNK, CHUNK), dtype=bool))
    wm = jnp.where(mask, w_s, 0).astype(F32)
    eye = jnp.eye(nb, dtype=F32)
    nslot = npos
    kron = []
    bsb = []
    for p in range(npos):
        per_slot = []
        for s in range(nslot):
            q = p - (nslot - 1 - s)
            if q < 0:
                blk = jnp.zeros((N_GROUPS_B, tt, tt), F32)
            else:
                blk = wm[:, p * tt:(p + 1) * tt, q * tt:(q + 1) * tt]
            k = jnp.einsum('gij,ab->giajb', blk, eye).reshape(N_GROUPS_B, tt * nb, tt * nb)
            per_slot.append(k)
        kron.append(jnp.stack(per_slot))
        bias = b_s[:, p * tt:(p + 1) * tt]
        bias = jnp.broadcast_to(bias.T[:, None, :, None], (tt, nb, N_GROUPS_B, GW_B))
        bsb.append(bias.reshape(tt * nb, D_B))
    return jnp.stack(kron).astype(BF16), jnp.stack(bsb).astype(F32)


def _ssm_weights(a_re, a_im, log_dt, b_re, b_im, c_re, c_im):
    a = lax.complex(a_re.astype(F32), a_im.astype(F32))
    dt = jnp.exp(log_dt.astype(F32))[:, None]
    a_bar = jnp.exp(a * dt)
    b = lax.complex(b_re.astype(F32), b_im.astype(F32))
    b_bar = ((a_bar - 1.0) / a)[:, :, None] * b
    gl = N_GROUPS_C // SSM_SUB
    eye = jnp.eye(gl, dtype=F32)

    def pack_b(part):
        part = part.reshape(SSM_SUB, gl, SSM_STATE, SSM_GROUP)
        return jnp.einsum('qgnh,gk->qghkn', part, eye).reshape(SSM_SUB, SUB_IN, SUB_ST)

    def pack_c(part):
        part = part.reshape(SSM_SUB, gl, SSM_GROUP, SSM_STATE)
        return jnp.einsum('qghn,gk->qgnkh', part, eye).reshape(SSM_SUB, SUB_ST, SUB_IN)

    bblk = jnp.concatenate([pack_b(jnp.real(b_bar)), pack_b(jnp.imag(b_bar))], axis=2)
    cblk = jnp.concatenate([pack_c(c_re.astype(F32)), -pack_c(c_im.astype(F32))], axis=1)
    return (jnp.real(a_bar).reshape(1, N_STATE), jnp.imag(a_bar).reshape(1, N_STATE),
            bblk.astype(BF16), cblk.astype(BF16))


def _to_tb(x):
    b, t, c = x.shape
    return jnp.swapaxes(x, 0, 1).reshape(t * b, c)


def _from_tb(x, b):
    tb, c = x.shape
    return jnp.swapaxes(x.reshape(tb // b, b, c), 0, 1)


def _trunk(x, p, buf_a, s_re, s_im, buf_f, layers, g_final, tt):
    nb, t_total, _ = x.shape
    nt = t_total // tt
    depth = len(layers)
    xr = _to_tb(x)
    la, lv, lre, lim, lf = [], [], [], [], []
    for i, w in enumerate(layers):
        wk = dict(w)
        wk['kron'], wk['bsb'] = w['kron_fn'](nb, tt, nt)
        xr, nh_a, cv, nre, nim = _mixer_call(
            nb, tt, nt, xr, _to_tb(buf_a[i]), s_re[i].reshape(nb, N_STATE),
            s_im[i].reshape(nb, N_STATE), wk)
        xr, nh_f = _ffn_call(nb, tt, nt, i == depth - 1, xr, _to_tb(p[i]).astype(BF16),
                             _to_tb(buf_f[i]), w, g_final)
        la.append(_from_tb(nh_a, nb))
        lv.append(_from_tb(cv, nb))
        lre.append(nre.reshape(nb, N_GROUPS_C, SSM_STATE))
        lim.append(nim.reshape(nb, N_GROUPS_C, SSM_STATE))
        lf.append(_from_tb(nh_f, nb))
    return (_from_tb(xr, nb), jnp.stack(la), jnp.stack(lv), jnp.stack(lre), jnp.stack(lim),
            jnp.stack(lf))


def kernel(x_prompt, x_sample, state_conv_a, state_ssm_re, state_ssm_im, state_conv_ffn,
           p_prompt, p_sample, g_mix, w_in, conv_a_w, conv_a_b, ln_a_g, ln_a_b, w_a_out,
           ln_b_g, ln_b_b, w_s, b_s, w_b_out, ssm_a_re, ssm_a_im, ssm_log_dt, ssm_b_re,
           ssm_b_im, ssm_c_re, ssm_c_im, ssm_d, w_c_glu, w_out, g_ffn, w_up, conv_f_w,
           conv_f_b, w_down, g_ple, w_pg, w_pe, g_final):
    depth = w_in.shape[0]
    row = lambda a: a.reshape(1, -1).astype(F32)
    layers = []
    for i in range(depth):
        a_re, a_im, bblk, cblk = _ssm_weights(ssm_a_re[i], ssm_a_im[i], ssm_log_dt[i],
                                              ssm_b_re[i], ssm_b_im[i], ssm_c_re[i], ssm_c_im[i])

        def kron_fn(nb, tt, nt, i=i):
            npos = min(max(CHUNK // tt, 1), nt)
            assert npos in (1, 2) and (tt >= CHUNK or CHUNK % tt == 0)
            return _kron_weights(w_s[i], b_s[i], nb, min(tt, CHUNK), npos)

        layers.append(dict(
            g_mix=row(g_mix[i]), w_in=w_in[i].astype(BF16), conv_a_w=conv_a_w[i].astype(F32),
            conv_a_b=row(conv_a_b[i]), ln_a_g=row(ln_a_g[i]), ln_a_b=row(ln_a_b[i]),
            w_a_out=w_a_out[i].astype(BF16), ln_b_g=row(ln_b_g[i]), ln_b_b=row(ln_b_b[i]),
            kron_fn=kron_fn, w_b_out=w_b_out[i].astype(BF16), a_re=a_re, a_im=a_im, bblk=bblk,
            cblk=cblk, ssm_d=row(ssm_d[i]), w_c_glu=w_c_glu[i].astype(BF16),
            w_out=w_out[i].astype(BF16), g_ffn=row(g_ffn[i]), w_up=w_up[i].astype(BF16),
            conv_f_w=conv_f_w[i].astype(F32), conv_f_b=row(conv_f_b[i]),
            w_down=w_down[i].astype(BF16), g_ple=row(g_ple[i]), w_pg=w_pg[i].astype(BF16),
            w_pe=w_pe[i].astype(BF16)))
    gfin = row(g_final)

    bp = x_prompt.shape[0]
    zero_a = jnp.zeros((depth, bp, HIST_A, D_A), F32)
    zero_s = jnp.zeros((depth, bp, N_GROUPS_C, SSM_STATE), F32)
    zero_f = jnp.zeros((depth, bp, HIST_F, 2 * D_FF), F32)
    y_p, ca_p, cv_p, re_p, im_p, cf_p = _trunk(
        x_prompt, p_prompt, zero_a, zero_s, zero_s, zero_f, layers, gfin, 64)
    y_s, ca_s, cv_s, re_s, im_s, cf_s = _trunk(
        x_sample, p_sample, state_conv_a, state_ssm_re, state_ssm_im, state_conv_ffn,
        layers, gfin, 1)
    return (y_p, y_s, ca_p, ca_s, cv_p, cv_s, re_p, im_p, re_s, im_s, cf_p, cf_s)
```

```python
import functools

import jax
import jax.numpy as jnp
from jax import lax
from jax.experimental import pallas as pl
from jax.experimental.pallas import tpu as pltpu

D_MODEL = 1024
D_PLE = 256
D_A = 512
D_B = 512
D_C = 512
CONV_A_WIDTH = 31
CHUNK = 128
N_GROUPS_B = 4
GW_B = D_B // N_GROUPS_B
SSM_GROUP = 16
N_GROUPS_C = D_C // SSM_GROUP
SSM_STATE = 64
D_FF = 2816
FFN_CONV_WIDTH = 3
EPS = 1e-6

N_STATE = N_GROUPS_C * SSM_STATE
SSM_SUB = 4
SUB_IN = D_C // SSM_SUB
SUB_ST = N_STATE // SSM_SUB
HIST_A = CONV_A_WIDTH - 1
HIST_F = FFN_CONV_WIDTH - 1
O_B = 2 * D_A
O_C = O_B + 2 * D_B
O_G = O_C + D_C
FF_CHUNKS = ((0, 1024), (1024, 1024), (2048, 768))
CONV_ROWS = 32
VMEM_LIMIT = 58 * 1024 * 1024

F32 = jnp.float32
BF16 = jnp.bfloat16


def _rms(x, g):
    ms = jnp.mean(x * x, axis=-1, keepdims=True)
    return x * lax.rsqrt(ms + EPS) * g


def _ln(x, g, b):
    mu = jnp.mean(x, axis=-1, keepdims=True)
    xc = x - mu
    var = jnp.mean(xc * xc, axis=-1, keepdims=True)
    return xc * lax.rsqrt(var + EPS) * g + b


def _dot(a, b):
    return jnp.dot(a, b, preferred_element_type=F32)


def _mixer_kernel(cfg, x_ref, hist_ref, s0re_ref, s0im_ref, kron_ref, bsb_ref, gmix_ref, win_ref,
                  caw_ref, cab_ref, lag_ref, lab_ref, waout_ref, lbg_ref, lbb_ref, wbout_ref,
                  are_ref, aim_ref, bblk_ref, cblk_ref, ssmd_ref, wcglu_ref, wout_ref,
                  x1_ref, nhist_ref, cv_ref, nsre_ref, nsim_ref,
                  hist_s, vprev_s, sst_s):
    nb, tt, npos = cfg
    rows = nb * tt
    hb = HIST_A * nb
    i = pl.program_id(0)

    @pl.when(i == 0)
    def _init():
        hist_s[...] = hist_ref[...]
        sst_s[0] = s0re_ref[...]
        sst_s[1] = s0im_ref[...]
        vprev_s[...] = jnp.zeros(vprev_s.shape, vprev_s.dtype)

    x = x_ref[...]
    h = _rms(x, gmix_ref[...]).astype(BF16)

    za = _dot(h, win_ref[:, 0:O_B])
    zc = _dot(h, win_ref[:, O_C:O_G])
    bus = [_dot(zc[:, q * SUB_IN:(q + 1) * SUB_IN].astype(BF16), bblk_ref[q])
           for q in range(SSM_SUB)]
    zb = _dot(h, win_ref[:, O_B:O_C])

    def gate(j):
        return jax.nn.sigmoid(_dot(h, win_ref[:, O_G + j * D_MODEL:O_G + (j + 1) * D_MODEL]))

    u = zb[:, :D_B]
    v = _ln(zb[:, D_B:], lbg_ref[...], lbb_ref[...])
    cv_ref[...] = v
    v16 = v.astype(BF16)
    slots = [v16] if npos == 1 else [vprev_s[...].astype(BF16), v16]
    vm_parts = []
    for g in range(N_GROUPS_B):
        cs = slice(g * GW_B, (g + 1) * GW_B)
        part = None
        for s, vs in enumerate(slots):
            t = _dot(kron_ref[0, s, g], vs[:, cs])
            part = t if part is None else part + t
        vm_parts.append(part)
    vm = jnp.concatenate(vm_parts, axis=1) + bsb_ref[0]
    if npos > 1:
        vprev_s[...] = v
    uvm = (u * vm).astype(BF16)
    g_a = gate(0)

    a = za[:, :D_A] * jax.nn.sigmoid(za[:, D_A:])
    cat = jnp.concatenate([hist_s[...], a], axis=0)
    new_hist = cat[rows:rows + hb]
    hist_s[...] = new_hist
    nhist_ref[...] = new_hist
    cab = cab_ref[...]
    conv_blocks = []
    rb = min(CONV_ROWS, rows)
    for r0 in range(0, rows, rb):
        acc = jnp.broadcast_to(cab, (rb, D_A))
        for k in range(CONV_A_WIDTH):
            acc = acc + caw_ref[k:k + 1, :] * cat[k * nb + r0:k * nb + r0 + rb]
        conv_blocks.append(acc)
    a_conv = jnp.concatenate(conv_blocks, axis=0) if len(conv_blocks) > 1 else conv_blocks[0]
    a_act = _ln(a_conv, lag_ref[...], lab_ref[...])
    a_act = (a_act * jax.nn.sigmoid(a_act)).astype(BF16)
    m = g_a * _dot(a_act, waout_ref[...])
    m = m + gate(1) * _dot(uvm, wbout_ref[...])

    y_parts = []
    for q in range(SSM_SUB):
        uq = zc[:, q * SUB_IN:(q + 1) * SUB_IN]
        bu = bus[q]
        st = slice(q * SUB_ST, (q + 1) * SUB_ST)
        ar = jnp.broadcast_to(are_ref[:, st], (nb, SUB_ST))
        ai = jnp.broadcast_to(aim_ref[:, st], (nb, SUB_ST))
        sr = sst_s[0, :, st]
        si = sst_s[1, :, st]
        s_rows = []
        for t in range(tt):
            bt = bu[t * nb:(t + 1) * nb]
            nr = ar * sr - ai * si + bt[:, :SUB_ST]
            ni = ar * si + ai * sr + bt[:, SUB_ST:]
            sr, si = nr, ni
            s_rows.append(jnp.concatenate([nr, ni], axis=1))
        sst_s[0, :, st] = sr
        sst_s[1, :, st] = si
        nsre_ref[:, st] = sr
        nsim_ref[:, st] = si
        s_all = jnp.concatenate(s_rows, axis=0) if tt > 1 else s_rows[0]
        yq = _dot(s_all.astype(BF16), cblk_ref[q]) + ssmd_ref[:, q * SUB_IN:(q + 1) * SUB_IN] * uq
        y_parts.append(jax.nn.gelu(yq).astype(BF16))
    yc = jnp.concatenate(y_parts, axis=1)
    g_c = gate(2)
    gc = _dot(yc, wcglu_ref[...])
    m = m + g_c * (gc[:, :D_MODEL] * jax.nn.sigmoid(gc[:, D_MODEL:]))

    x1_ref[...] = x + _dot(m.astype(BF16), wout_ref[...])


def _ffn_kernel(cfg, x_ref, p_ref, hist_ref, gfin_ref, gffn_ref, wup_ref, cfw_ref, cfb_ref,
                wdown_ref, gple_ref, wpg_ref, wpe_ref,
                xo_ref, nhist_ref,
                hist_s):
    nb, tt, final = cfg
    rows = nb * tt
    hb = HIST_F * nb
    i = pl.program_id(0)

    @pl.when(i == 0)
    def _init():
        hist_s[...] = hist_ref[...]

    x = x_ref[...]
    h = _rms(x, gffn_ref[...]).astype(BF16)
    f = None
    for c0, cw in FF_CHUNKS:
        halves = []
        for off in (c0, D_FF + c0):
            up = _dot(h, wup_ref[:, off:off + cw])
            cat = jnp.concatenate([hist_s[:, off:off + cw], up], axis=0)
            new_hist = cat[rows:rows + hb]
            hist_s[:, off:off + cw] = new_hist
            nhist_ref[:, off:off + cw] = new_hist
            y = cfb_ref[:, off:off + cw] + cfw_ref[0:1, off:off + cw] * cat[0:rows]
            y = y + cfw_ref[1:2, off:off + cw] * cat[nb:nb + rows]
            y = y + cfw_ref[2:3, off:off + cw] * up
            halves.append(y)
        act = (jax.nn.gelu(halves[0]) * halves[1]).astype(BF16)
        t = _dot(act, wdown_ref[c0:c0 + cw, :])
        f = t if f is None else f + t
    x2 = x + f
    h3 = _rms(x2, gple_ref[...]).astype(BF16)
    gate = jax.nn.sigmoid(_dot(h3, wpg_ref[...]))
    x3 = x2 + gate * _dot(p_ref[...], wpe_ref[...])
    if final:
        x3 = _rms(x3, gfin_ref[...])
    xo_ref[...] = x3


def _const_spec(shape):
    nd = len(shape)
    return pl.BlockSpec(shape, lambda i: (0,) * nd, pipeline_mode=pl.Buffered(1))


def _layer_spec(shape, layer):
    nd = len(shape)
    return pl.BlockSpec((None,) + tuple(shape[1:]), lambda i: (layer,) + (0,) * (nd - 1),
                        pipeline_mode=pl.Buffered(1))


def _pos_spec(shape, npos):
    nd = len(shape)
    return pl.BlockSpec((1,) + tuple(shape[1:]), lambda i: (i % npos,) + (0,) * (nd - 1))


def _out_const_spec(shape):
    nd = len(shape)
    return pl.BlockSpec(shape, lambda i: (0,) * nd)


MIXER_PARAMS = ('g_mix', 'w_in', 'conv_a_w', 'conv_a_b', 'ln_a_g', 'ln_a_b', 'w_a_out', 'ln_b_g',
                'ln_b_b', 'w_b_out', 'a_re', 'a_im', 'bblk', 'cblk', 'ssm_d', 'w_c_glu', 'w_out')
FFN_PARAMS = ('g_ffn', 'w_up', 'conv_f_w', 'conv_f_b', 'w_down', 'g_ple', 'w_pg', 'w_pe')


def _mixer_call(layer, nb, tt, nt, x, hist0, s0re, s0im, kron, bsb, w):
    rows = nb * tt
    t_total = tt * nt
    npos = kron.shape[0]
    last_start = ((t_total - 1) // CHUNK) * CHUNK
    n_last = (t_total - last_start) // tt
    stacked = [w[k] for k in MIXER_PARAMS]
    in_specs = [pl.BlockSpec((rows, D_MODEL), lambda i: (i, 0)),
                _const_spec(hist0.shape), _const_spec(s0re.shape), _const_spec(s0im.shape),
                _pos_spec(kron.shape, npos), _pos_spec(bsb.shape, npos)]
    in_specs += [_layer_spec(c.shape, layer) for c in stacked]
    out_shape = [
        jax.ShapeDtypeStruct((nt * rows, D_MODEL), F32),
        jax.ShapeDtypeStruct((HIST_A * nb, D_A), F32),
        jax.ShapeDtypeStruct((n_last * rows, D_B), F32),
        jax.ShapeDtypeStruct((nb, N_STATE), F32),
        jax.ShapeDtypeStruct((nb, N_STATE), F32),
    ]
    out_specs = [
        pl.BlockSpec((rows, D_MODEL), lambda i: (i, 0)),
        _out_const_spec((HIST_A * nb, D_A)),
        pl.BlockSpec((rows, D_B), lambda i: (jnp.maximum(i - (nt - n_last), 0), 0)),
        _out_const_spec((nb, N_STATE)),
        _out_const_spec((nb, N_STATE)),
    ]
    scratch = [
        pltpu.VMEM((HIST_A * nb, D_A), F32),
        pltpu.VMEM((rows, D_B), F32),
        pltpu.VMEM((2, nb, N_STATE), F32),
    ]
    return pl.pallas_call(
        functools.partial(_mixer_kernel, (nb, tt, npos)),
        grid=(nt,),
        in_specs=in_specs,
        out_specs=out_specs,
        out_shape=out_shape,
        scratch_shapes=scratch,
        compiler_params=pltpu.CompilerParams(
            dimension_semantics=("arbitrary",), vmem_limit_bytes=VMEM_LIMIT),
        name=f"mixer_b{nb}_t{tt}",
    )(x, hist0, s0re, s0im, kron, bsb, *stacked)


def _ffn_call(layer, nb, tt, nt, final, x, p, hist0, g_final, w):
    rows = nb * tt
    stacked = [w[k] for k in FFN_PARAMS]
    in_specs = [pl.BlockSpec((rows, D_MODEL), lambda i: (i, 0)),
                pl.BlockSpec((None, rows, D_PLE), lambda i: (layer, i, 0)),
                _const_spec(hist0.shape), _const_spec(g_final.shape)]
    in_specs += [_layer_spec(c.shape, layer) for c in stacked]
    out_shape = [
        jax.ShapeDtypeStruct((nt * rows, D_MODEL), F32),
        jax.ShapeDtypeStruct((HIST_F * nb, 2 * D_FF), F32),
    ]
    out_specs = [
        pl.BlockSpec((rows, D_MODEL), lambda i: (i, 0)),
        _out_const_spec((HIST_F * nb, 2 * D_FF)),
    ]
    return pl.pallas_call(
        functools.partial(_ffn_kernel, (nb, tt, final)),
        grid=(nt,),
        in_specs=in_specs,
        out_specs=out_specs,
        out_shape=out_shape,
        scratch_shapes=[pltpu.VMEM((HIST_F * nb, 2 * D_FF), F32)],
        compiler_params=pltpu.CompilerParams(
            dimension_semantics=("arbitrary",), vmem_limit_bytes=VMEM_LIMIT),
        name=f"ffn_b{nb}_t{tt}",
    )(x, p, hist0, g_final, *stacked)


def _kron_weights(w_s, b_s, nb, tt, npos):
    mask = jnp.tril(jnp.ones((CHUNK, CHUNK), dtype=bool))
    wm = jnp.where(mask, w_s, 0).astype(BF16)
    r = jnp.arange(tt * nb)
    expand = (r[:, None] // nb == jnp.arange(tt)[None, :]).astype(BF16)
    same_batch = (r[:, None] % nb) == (r[None, :] % nb)
    kron = []
    bsb = []
    for p in range(npos):
        per_slot = []
        for s in range(npos):
            q = p - (npos - 1 - s)
            if q < 0:
                per_slot.append(jnp.zeros((N_GROUPS_B, tt * nb, tt * nb), BF16))
                continue
            blk = wm[:, p * tt:(p + 1) * tt, q * tt:(q + 1) * tt]
            k = jnp.einsum('ri,gij->grj', expand, blk, preferred_element_type=F32).astype(BF16)
            k = jnp.einsum('grj,cj->grc', k, expand, preferred_element_type=F32)
            per_slot.append(jnp.where(same_batch, k, 0).astype(BF16))
        kron.append(jnp.stack(per_slot))
        bias = b_s[:, p * tt:(p + 1) * tt]
        bias = jnp.broadcast_to(bias.T[:, None, :, None], (tt, nb, N_GROUPS_B, GW_B))
        bsb.append(bias.reshape(tt * nb, D_B))
    return jnp.stack(kron), jnp.stack(bsb).astype(F32)


def _ssm_weights(a_re, a_im, log_dt, b_re, b_im, c_re, c_im):
    depth = a_re.shape[0]
    a = lax.complex(a_re.astype(F32), a_im.astype(F32))
    dt = jnp.exp(log_dt.astype(F32))[:, :, None]
    a_bar = jnp.exp(a * dt)
    b = lax.complex(b_re.astype(F32), b_im.astype(F32))
    b_bar = ((a_bar - 1.0) / a)[..., None] * b
    gl = N_GROUPS_C // SSM_SUB
    eye = jnp.eye(gl, dtype=F32)

    def pack_b(part):
        part = part.reshape(depth, SSM_SUB, gl, SSM_STATE, SSM_GROUP)
        return jnp.einsum('lqgnh,gk->lqghkn', part, eye).reshape(depth, SSM_SUB, SUB_IN, SUB_ST)

    def pack_c(part):
        part = part.reshape(depth, SSM_SUB, gl, SSM_GROUP, SSM_STATE)
        return jnp.einsum('lqghn,gk->lqgnkh', part, eye).reshape(depth, SSM_SUB, SUB_ST, SUB_IN)

    bblk = jnp.concatenate([pack_b(jnp.real(b_bar)), pack_b(jnp.imag(b_bar))], axis=3)
    cblk = jnp.concatenate([pack_c(c_re.astype(F32)), -pack_c(c_im.astype(F32))], axis=2)
    return (jnp.real(a_bar).reshape(depth, 1, N_STATE), jnp.imag(a_bar).reshape(depth, 1, N_STATE),
            bblk.astype(BF16), cblk.astype(BF16))


def _to_tb(x):
    *lead, b, t, c = x.shape
    return jnp.swapaxes(x, -3, -2).reshape(*lead, t * b, c)


def _from_tb(x, b):
    *lead, tb, c = x.shape
    return jnp.swapaxes(x.reshape(*lead, tb // b, b, c), -3, -2)


def _trunk(x, p, buf_a, s_re, s_im, buf_f, w, w_s, b_s, g_final, tt):
    nb, t_total, _ = x.shape
    nt = t_total // tt
    depth = p.shape[0]
    npos = min(max(CHUNK // tt, 1), nt)
    assert npos in (1, 2) and tt <= CHUNK and CHUNK % tt == 0
    xr = _to_tb(x)
    pr = _to_tb(p).astype(BF16)
    hist_a = _to_tb(buf_a)
    hist_f = _to_tb(buf_f)
    s_re = s_re.reshape(depth, nb, N_STATE)
    s_im = s_im.reshape(depth, nb, N_STATE)
    la, lv, lre, lim, lf = [], [], [], [], []
    for i in range(depth):
        kron, bsb = _kron_weights(w_s[i], b_s[i], nb, tt, npos)
        xr, nh_a, cv, nre, nim = _mixer_call(i, nb, tt, nt, xr, hist_a[i], s_re[i], s_im[i],
                                             kron, bsb, w)
        xr, nh_f = _ffn_call(i, nb, tt, nt, i == depth - 1, xr, pr, hist_f[i], g_final, w)
        la.append(nh_a)
        lv.append(cv)
        lre.append(nre)
        lim.append(nim)
        lf.append(nh_f)
    state = lambda parts: jnp.stack(parts).reshape(depth, nb, N_GROUPS_C, SSM_STATE)
    return (_from_tb(xr, nb), _from_tb(jnp.stack(la), nb), _from_tb(jnp.stack(lv), nb),
            state(lre), state(lim), _from_tb(jnp.stack(lf), nb))


def kernel(x_prompt, x_sample, state_conv_a, state_ssm_re, state_ssm_im, state_conv_ffn,
           p_prompt, p_sample, g_mix, w_in, conv_a_w, conv_a_b, ln_a_g, ln_a_b, w_a_out,
           ln_b_g, ln_b_b, w_s, b_s, w_b_out, ssm_a_re, ssm_a_im, ssm_log_dt, ssm_b_re,
           ssm_b_im, ssm_c_re, ssm_c_im, ssm_d, w_c_glu, w_out, g_ffn, w_up, conv_f_w,
           conv_f_b, w_down, g_ple, w_pg, w_pe, g_final):
    depth = w_in.shape[0]
    row = lambda a: a.reshape(depth, 1, -1).astype(F32)
    a_re, a_im, bblk, cblk = _ssm_weights(ssm_a_re, ssm_a_im, ssm_log_dt, ssm_b_re, ssm_b_im,
                                          ssm_c_re, ssm_c_im)
    w = dict(
        g_mix=row(g_mix), w_in=w_in.astype(BF16), conv_a_w=conv_a_w.astype(F32),
        conv_a_b=row(conv_a_b), ln_a_g=row(ln_a_g), ln_a_b=row(ln_a_b),
        w_a_out=w_a_out.astype(BF16), ln_b_g=row(ln_b_g), ln_b_b=row(ln_b_b),
        w_b_out=w_b_out.astype(BF16), a_re=a_re, a_im=a_im, bblk=bblk, cblk=cblk,
        ssm_d=row(ssm_d), w_c_glu=w_c_glu.astype(BF16), w_out=w_out.astype(BF16),
        g_ffn=row(g_ffn), w_up=w_up.astype(BF16), conv_f_w=conv_f_w.astype(F32),
        conv_f_b=row(conv_f_b), w_down=w_down.astype(BF16), g_ple=row(g_ple),
        w_pg=w_pg.astype(BF16), w_pe=w_pe.astype(BF16))
    gfin = g_final.reshape(1, -1).astype(F32)

    bp = x_prompt.shape[0]
    zero_a = jnp.zeros((depth, bp, HIST_A, D_A), F32)
    zero_s = jnp.zeros((depth, bp, N_GROUPS_C, SSM_STATE), F32)
    zero_f = jnp.zeros((depth, bp, HIST_F, 2 * D_FF), F32)
    y_p, ca_p, cv_p, re_p, im_p, cf_p = _trunk(
        x_prompt, p_prompt, zero_a, zero_s, zero_s, zero_f, w, w_s, b_s, gfin, 64)
    y_s, ca_s, cv_s, re_s, im_s, cf_s = _trunk(
        x_sample, p_sample, state_conv_a, state_ssm_re, state_ssm_im, state_conv_ffn,
        w, w_s, b_s, gfin, 1)
    return (y_p, y_s, ca_p, ca_s, cv_p, cv_s, re_p, im_p, re_s, im_s, cf_p, cf_s)
```

```python
import functools

import jax
import jax.numpy as jnp
from jax import lax
from jax.experimental import pallas as pl
from jax.experimental.pallas import tpu as pltpu

D_MODEL = 1024
D_PLE = 256
D_A = 512
D_B = 512
D_C = 512
CONV_A_WIDTH = 31
CHUNK = 128
N_GROUPS_B = 4
GW_B = D_B // N_GROUPS_B
SSM_GROUP = 16
N_GROUPS_C = D_C // SSM_GROUP
SSM_STATE = 64
D_FF = 2816
FFN_CONV_WIDTH = 3
EPS = 1e-6

N_STATE = N_GROUPS_C * SSM_STATE
SSM_SUB = 4
SUB_IN = D_C // SSM_SUB
SUB_ST = N_STATE // SSM_SUB
HIST_A = CONV_A_WIDTH - 1
HIST_F = FFN_CONV_WIDTH - 1
O_B = 2 * D_A
O_C = O_B + 2 * D_B
O_G = O_C + D_C
FF_CHUNKS = ((0, 1024), (1024, 1024), (2048, 768))
CONV_ROWS = 32
VMEM_LIMIT = 58 * 1024 * 1024

F32 = jnp.float32
BF16 = jnp.bfloat16


def _rms(x, g):
    ms = jnp.mean(x * x, axis=-1, keepdims=True)
    return x * lax.rsqrt(ms + EPS) * g


def _ln(x, g, b):
    mu = jnp.mean(x, axis=-1, keepdims=True)
    xc = x - mu
    var = jnp.mean(xc * xc, axis=-1, keepdims=True)
    return xc * lax.rsqrt(var + EPS) * g + b


def _dot(a, b):
    return jnp.dot(a, b, preferred_element_type=F32)


def _load_rows(ref, nb, tt):
    v = ref[...]
    if v.ndim == 3:
        v = jnp.swapaxes(v, 0, 1).reshape(tt * nb, v.shape[-1])
    return v


def _build_kron(ws_ref, kron_s, nb, tt, npos):
    rows = nb * tt
    shift = nb.bit_length() - 1
    assert nb == 1 << shift
    r_i = lax.broadcasted_iota(jnp.int32, (rows, rows), 0)
    c_i = lax.broadcasted_iota(jnp.int32, (rows, rows), 1)
    same_batch = (r_i & (nb - 1)) == (c_i & (nb - 1))
    blocks = [(0, 0)] if npos == 1 else [(0, 0), (1, 0), (1, 1)]
    if tt > 1:
        e_r = lax.shift_right_logical(lax.broadcasted_iota(jnp.int32, (rows, tt), 0), shift)
        expand = jnp.where(e_r == lax.broadcasted_iota(jnp.int32, (rows, tt), 1), 1.0, 0.0)
        e_c = lax.shift_right_logical(lax.broadcasted_iota(jnp.int32, (tt, rows), 1), shift)
        expand_t = jnp.where(e_c == lax.broadcasted_iota(jnp.int32, (tt, rows), 0), 1.0, 0.0)
        expand, expand_t = expand.astype(BF16), expand_t.astype(BF16)
        tril = (lax.broadcasted_iota(jnp.int32, (tt, tt), 1)
                <= lax.broadcasted_iota(jnp.int32, (tt, tt), 0))
    for slab, (p, q) in enumerate(blocks):
        for g in range(N_GROUPS_B):
            blk = ws_ref[g, p * tt:(p + 1) * tt, q * tt:(q + 1) * tt]
            if tt == 1:
                k = jnp.broadcast_to(blk, (rows, rows))
            else:
                if p == q:
                    blk = jnp.where(tril, blk, 0.0)
                k = _dot(_dot(expand, blk.astype(BF16)).astype(BF16), expand_t)
            kron_s[slab, g] = jnp.where(same_batch, k, 0.0).astype(BF16)


def _mixer_kernel(cfg, x_ref, hist_ref, s0re_ref, s0im_ref, bsb_ref, ws_ref, gmix_ref, win_ref,
                  caw_ref, cab_ref, lag_ref, lab_ref, waout_ref, lbg_ref, lbb_ref, wbout_ref,
                  are_ref, aim_ref, bblk_ref, cblk_ref, ssmd_ref, wcglu_ref, wout_ref,
                  x1_ref, nhist_ref, cv_ref, nsre_ref, nsim_ref,
                  hist_s, vprev_s, sst_s, kron_s):
    nb, tt, npos = cfg
    rows = nb * tt
    hb = HIST_A * nb
    i = pl.program_id(0)

    @pl.when(i == 0)
    def _init():
        hist_s[...] = hist_ref[...]
        sst_s[0] = s0re_ref[...]
        sst_s[1] = s0im_ref[...]
        vprev_s[...] = jnp.zeros(vprev_s.shape, vprev_s.dtype)
        _build_kron(ws_ref, kron_s, nb, tt, npos)

    x = _load_rows(x_ref, nb, tt)
    h = _rms(x, gmix_ref[...]).astype(BF16)

    za = _dot(h, win_ref[:, 0:O_B])
    zc = _dot(h, win_ref[:, O_C:O_G])
    bus = [_dot(zc[:, q * SUB_IN:(q + 1) * SUB_IN].astype(BF16), bblk_ref[q])
           for q in range(SSM_SUB)]
    zb = _dot(h, win_ref[:, O_B:O_C])

    def gate(j):
        return jax.nn.sigmoid(_dot(h, win_ref[:, O_G + j * D_MODEL:O_G + (j + 1) * D_MODEL]))

    u = zb[:, :D_B]
    v = _ln(zb[:, D_B:], lbg_ref[...], lbb_ref[...])
    cv_ref[...] = v
    v16 = v.astype(BF16)
    vm_parts = []
    if npos == 1:
        for g in range(N_GROUPS_B):
            cs = slice(g * GW_B, (g + 1) * GW_B)
            vm_parts.append(_dot(kron_s[0, g], v16[:, cs]))
    else:
        pos = i % 2
        vprev = vprev_s[...].astype(BF16)
        for g in range(N_GROUPS_B):
            cs = slice(g * GW_B, (g + 1) * GW_B)
            vm_parts.append(_dot(kron_s[1, g], vprev[:, cs]) + _dot(kron_s[2 * pos, g], v16[:, cs]))
    bias_rows = [jnp.broadcast_to(bsb_ref[0, t:t + 1, :], (nb, D_B)) for t in range(tt)]
    bias = jnp.concatenate(bias_rows, axis=0) if tt > 1 else bias_rows[0]
    vm = jnp.concatenate(vm_parts, axis=1) + bias
    if npos > 1:
        vprev_s[...] = jnp.where(pos == 0, v, 0.0)
    uvm = (u * vm).astype(BF16)
    g_a = gate(0)

    a = za[:, :D_A] * jax.nn.sigmoid(za[:, D_A:])
    cat = jnp.concatenate([hist_s[...], a], axis=0)
    new_hist = cat[rows:rows + hb]
    hist_s[...] = new_hist
    nhist_ref[...] = new_hist
    cab = cab_ref[...]
    conv_blocks = []
    rb = min(CONV_ROWS, rows)
    for r0 in range(0, rows, rb):
        acc = jnp.broadcast_to(cab, (rb, D_A))
        for k in range(CONV_A_WIDTH):
            acc = acc + caw_ref[k:k + 1, :] * cat[k * nb + r0:k * nb + r0 + rb]
        conv_blocks.append(acc)
    a_conv = jnp.concatenate(conv_blocks, axis=0) if len(conv_blocks) > 1 else conv_blocks[0]
    a_act = _ln(a_conv, lag_ref[...], lab_ref[...])
    a_act = (a_act * jax.nn.sigmoid(a_act)).astype(BF16)
    m = g_a * _dot(a_act, waout_ref[...])
    m = m + gate(1) * _dot(uvm, wbout_ref[...])

    y_parts = []
    for q in range(SSM_SUB):
        uq = zc[:, q * SUB_IN:(q + 1) * SUB_IN]
        bu = bus[q]
        st = slice(q * SUB_ST, (q + 1) * SUB_ST)
        ar = jnp.broadcast_to(are_ref[:, st], (nb, SUB_ST))
        ai = jnp.broadcast_to(aim_ref[:, st], (nb, SUB_ST))
        sr = sst_s[0, :, st]
        si = sst_s[1, :, st]
        s_rows = []
        for t in range(tt):
            bt = bu[t * nb:(t + 1) * nb]
            nr = ar * sr - ai * si + bt[:, :SUB_ST]
            ni = ar * si + ai * sr + bt[:, SUB_ST:]
            sr, si = nr, ni
            s_rows.append(jnp.concatenate([nr, ni], axis=1))
        sst_s[0, :, st] = sr
        sst_s[1, :, st] = si
        nsre_ref[:, st] = sr
        nsim_ref[:, st] = si
        s_all = jnp.concatenate(s_rows, axis=0) if tt > 1 else s_rows[0]
        yq = _dot(s_all.astype(BF16), cblk_ref[q]) + ssmd_ref[:, q * SUB_IN:(q + 1) * SUB_IN] * uq
        y_parts.append(jax.nn.gelu(yq).astype(BF16))
    yc = jnp.concatenate(y_parts, axis=1)
    g_c = gate(2)
    gc = _dot(yc, wcglu_ref[...])
    m = m + g_c * (gc[:, :D_MODEL] * jax.nn.sigmoid(gc[:, D_MODEL:]))

    x1_ref[...] = x + _dot(m.astype(BF16), wout_ref[...])


def _ffn_kernel(cfg, x_ref, p_ref, hist_ref, gfin_ref, gffn_ref, wup_ref, cfw_ref, cfb_ref,
                wdown_ref, gple_ref, wpg_ref, wpe_ref,
                xo_ref, nhist_ref,
                hist_s):
    nb, tt, final = cfg
    rows = nb * tt
    hb = HIST_F * nb
    i = pl.program_id(0)

    @pl.when(i == 0)
    def _init():
        hist_s[...] = hist_ref[...]

    x = x_ref[...]
    p = _load_rows(p_ref, nb, tt).astype(BF16)
    h = _rms(x, gffn_ref[...]).astype(BF16)
    f = None
    for c0, cw in FF_CHUNKS:
        halves = []
        for off in (c0, D_FF + c0):
            up = _dot(h, wup_ref[:, off:off + cw])
            cat = jnp.concatenate([hist_s[:, off:off + cw], up], axis=0)
            new_hist = cat[rows:rows + hb]
            hist_s[:, off:off + cw] = new_hist
            nhist_ref[:, off:off + cw] = new_hist
            y = cfb_ref[:, off:off + cw] + cfw_ref[0:1, off:off + cw] * cat[0:rows]
            y = y + cfw_ref[1:2, off:off + cw] * cat[nb:nb + rows]
            y = y + cfw_ref[2:3, off:off + cw] * up
            halves.append(y)
        act = (jax.nn.gelu(halves[0]) * halves[1]).astype(BF16)
        t = _dot(act, wdown_ref[c0:c0 + cw, :])
        f = t if f is None else f + t
    x2 = x + f
    h3 = _rms(x2, gple_ref[...]).astype(BF16)
    gate = jax.nn.sigmoid(_dot(h3, wpg_ref[...]))
    x3 = x2 + gate * _dot(p, wpe_ref[...])
    if final:
        x3 = _rms(x3, gfin_ref[...])
    if len(xo_ref.shape) == 3:
        x3 = jnp.swapaxes(x3.reshape(tt, nb, D_MODEL), 0, 1)
    xo_ref[...] = x3


def _const_spec(shape):
    nd = len(shape)
    return pl.BlockSpec(shape, lambda i: (0,) * nd, pipeline_mode=pl.Buffered(1))


def _layer_spec(shape, layer):
    nd = len(shape)
    return pl.BlockSpec((None,) + tuple(shape[1:]), lambda i: (layer,) + (0,) * (nd - 1),
                        pipeline_mode=pl.Buffered(1))


def _pos_spec(shape, npos):
    nd = len(shape)
    return pl.BlockSpec((1,) + tuple(shape[1:]), lambda i: (i % npos,) + (0,) * (nd - 1))


def _out_const_spec(shape):
    nd = len(shape)
    return pl.BlockSpec(shape, lambda i: (0,) * nd)


MIXER_PARAMS = ('w_s', 'g_mix', 'w_in', 'conv_a_w', 'conv_a_b', 'ln_a_g', 'ln_a_b', 'w_a_out', 'ln_b_g',
                'ln_b_b', 'w_b_out', 'a_re', 'a_im', 'bblk', 'cblk', 'ssm_d', 'w_c_glu', 'w_out')
FFN_PARAMS = ('g_ffn', 'w_up', 'conv_f_w', 'conv_f_b', 'w_down', 'g_ple', 'w_pg', 'w_pe')


def _mixer_call(layer, nb, tt, nt, x, hist0, s0re, s0im, bsb, w):
    rows = nb * tt
    t_total = tt * nt
    npos = bsb.shape[0]
    last_start = ((t_total - 1) // CHUNK) * CHUNK
    n_last = (t_total - last_start) // tt
    stacked = [w[k] for k in MIXER_PARAMS]
    x_spec = (pl.BlockSpec((nb, tt, D_MODEL), lambda i: (0, i, 0)) if x.ndim == 3
              else pl.BlockSpec((rows, D_MODEL), lambda i: (i, 0)))
    in_specs = [x_spec,
                _const_spec(hist0.shape), _const_spec(s0re.shape), _const_spec(s0im.shape),
                _pos_spec(bsb.shape, npos)]
    in_specs += [_layer_spec(c.shape, layer) for c in stacked]
    out_shape = [
        jax.ShapeDtypeStruct((nt * rows, D_MODEL), F32),
        jax.ShapeDtypeStruct((HIST_A * nb, D_A), F32),
        jax.ShapeDtypeStruct((n_last * rows, D_B), F32),
        jax.ShapeDtypeStruct((nb, N_STATE), F32),
        jax.ShapeDtypeStruct((nb, N_STATE), F32),
    ]
    out_specs = [
        pl.BlockSpec((rows, D_MODEL), lambda i: (i, 0)),
        _out_const_spec((HIST_A * nb, D_A)),
        pl.BlockSpec((rows, D_B), lambda i: (jnp.maximum(i - (nt - n_last), 0), 0)),
        _out_const_spec((nb, N_STATE)),
        _out_const_spec((nb, N_STATE)),
    ]
    scratch = [
        pltpu.VMEM((HIST_A * nb, D_A), F32),
        pltpu.VMEM((rows, D_B), F32),
        pltpu.VMEM((2, nb, N_STATE), F32),
        pltpu.VMEM((1 if npos == 1 else 3, N_GROUPS_B, rows, rows), BF16),
    ]
    return pl.pallas_call(
        functools.partial(_mixer_kernel, (nb, tt, npos)),
        grid=(nt,),
        in_specs=in_specs,
        out_specs=out_specs,
        out_shape=out_shape,
        scratch_shapes=scratch,
        compiler_params=pltpu.CompilerParams(
            dimension_semantics=("arbitrary",), vmem_limit_bytes=VMEM_LIMIT),
        name=f"mixer_b{nb}_t{tt}",
    )(x, hist0, s0re, s0im, bsb, *stacked)


def _ffn_call(layer, nb, tt, nt, final, out_bt, x, p, hist0, g_final, w):
    rows = nb * tt
    stacked = [w[k] for k in FFN_PARAMS]
    p_spec = (pl.BlockSpec((None, nb, tt, D_PLE), lambda i: (layer, 0, i, 0)) if p.ndim == 4
              else pl.BlockSpec((None, rows, D_PLE), lambda i: (layer, i, 0)))
    in_specs = [pl.BlockSpec((rows, D_MODEL), lambda i: (i, 0)), p_spec,
                _const_spec(hist0.shape), _const_spec(g_final.shape)]
    in_specs += [_layer_spec(c.shape, layer) for c in stacked]
    if out_bt:
        x_out = jax.ShapeDtypeStruct((nb, nt * tt, D_MODEL), F32)
        x_out_spec = pl.BlockSpec((nb, tt, D_MODEL), lambda i: (0, i, 0))
    else:
        x_out = jax.ShapeDtypeStruct((nt * rows, D_MODEL), F32)
        x_out_spec = pl.BlockSpec((rows, D_MODEL), lambda i: (i, 0))
    out_shape = [x_out, jax.ShapeDtypeStruct((HIST_F * nb, 2 * D_FF), F32)]
    out_specs = [x_out_spec, _out_const_spec((HIST_F * nb, 2 * D_FF))]
    return pl.pallas_call(
        functools.partial(_ffn_kernel, (nb, tt, final)),
        grid=(nt,),
        in_specs=in_specs,
        out_specs=out_specs,
        out_shape=out_shape,
        scratch_shapes=[pltpu.VMEM((HIST_F * nb, 2 * D_FF), F32)],
        compiler_params=pltpu.CompilerParams(
            dimension_semantics=("arbitrary",), vmem_limit_bytes=VMEM_LIMIT),
        name=f"ffn_b{nb}_t{tt}",
    )(x, p, hist0, g_final, *stacked)


def _mix_bias_rows(b_s, tt, npos):
    bsb = []
    for p in range(npos):
        bias = b_s[:, p * tt:(p + 1) * tt]
        bias = jnp.broadcast_to(bias.T[:, :, None], (tt, N_GROUPS_B, GW_B))
        bsb.append(bias.reshape(tt, D_B))
    return jnp.stack(bsb).astype(F32)


def _ssm_weights(a_re, a_im, log_dt, b_re, b_im, c_re, c_im):
    depth = a_re.shape[0]
    a = lax.complex(a_re.astype(F32), a_im.astype(F32))
    dt = jnp.exp(log_dt.astype(F32))[:, :, None]
    a_bar = jnp.exp(a * dt)
    b = lax.complex(b_re.astype(F32), b_im.astype(F32))
    b_bar = ((a_bar - 1.0) / a)[..., None] * b
    gl = N_GROUPS_C // SSM_SUB
    eye = jnp.eye(gl, dtype=F32)

    def pack_b(part):
        part = part.reshape(depth, SSM_SUB, gl, SSM_STATE, SSM_GROUP)
        return jnp.einsum('lqgnh,gk->lqghkn', part, eye).reshape(depth, SSM_SUB, SUB_IN, SUB_ST)

    def pack_c(part):
        part = part.reshape(depth, SSM_SUB, gl, SSM_GROUP, SSM_STATE)
        return jnp.einsum('lqghn,gk->lqgnkh', part, eye).reshape(depth, SSM_SUB, SUB_ST, SUB_IN)

    bblk = jnp.concatenate([pack_b(jnp.real(b_bar)), pack_b(jnp.imag(b_bar))], axis=3)
    cblk = jnp.concatenate([pack_c(c_re.astype(F32)), -pack_c(c_im.astype(F32))], axis=2)
    return (jnp.real(a_bar).reshape(depth, 1, N_STATE), jnp.imag(a_bar).reshape(depth, 1, N_STATE),
            bblk.astype(BF16), cblk.astype(BF16))


def _to_tb(x):
    *lead, b, t, c = x.shape
    return jnp.swapaxes(x, -3, -2).reshape(*lead, t * b, c)


def _from_tb(x, b):
    *lead, tb, c = x.shape
    return jnp.swapaxes(x.reshape(*lead, tb // b, b, c), -3, -2)


def _trunk(x, p, buf_a, s_re, s_im, buf_f, w, b_s, g_final, tt):
    nb, t_total, _ = x.shape
    nt = t_total // tt
    depth = p.shape[0]
    npos = min(max(CHUNK // tt, 1), nt)
    assert npos in (1, 2) and tt <= CHUNK and CHUNK % tt == 0
    if tt > 1:
        xr, pr = x, p
    else:
        xr, pr = x.reshape(nb, D_MODEL), p.reshape(depth, nb, D_PLE)
    hist_a = _to_tb(buf_a)
    hist_f = _to_tb(buf_f)
    s_re = s_re.reshape(depth, nb, N_STATE)
    s_im = s_im.reshape(depth, nb, N_STATE)
    la, lv, lre, lim, lf = [], [], [], [], []
    for i in range(depth):
        bsb = _mix_bias_rows(b_s[i], tt, npos)
        xr, nh_a, cv, nre, nim = _mixer_call(i, nb, tt, nt, xr, hist_a[i], s_re[i], s_im[i],
                                             bsb, w)
        last = i == depth - 1
        xr, nh_f = _ffn_call(i, nb, tt, nt, last, last and tt > 1, xr, pr, hist_f[i], g_final, w)
        la.append(nh_a)
        lv.append(cv)
        lre.append(nre)
        lim.append(nim)
        lf.append(nh_f)
    state = lambda parts: jnp.stack(parts).reshape(depth, nb, N_GROUPS_C, SSM_STATE)
    return (xr.reshape(nb, t_total, D_MODEL), _from_tb(jnp.stack(la), nb),
            _from_tb(jnp.stack(lv), nb),
            state(lre), state(lim), _from_tb(jnp.stack(lf), nb))


def kernel(x_prompt, x_sample, state_conv_a, state_ssm_re, state_ssm_im, state_conv_ffn,
           p_prompt, p_sample, g_mix, w_in, conv_a_w, conv_a_b, ln_a_g, ln_a_b, w_a_out,
           ln_b_g, ln_b_b, w_s, b_s, w_b_out, ssm_a_re, ssm_a_im, ssm_log_dt, ssm_b_re,
           ssm_b_im, ssm_c_re, ssm_c_im, ssm_d, w_c_glu, w_out, g_ffn, w_up, conv_f_w,
           conv_f_b, w_down, g_ple, w_pg, w_pe, g_final):
    depth = w_in.shape[0]
    row = lambda a: a.reshape(depth, 1, -1).astype(F32)
    a_re, a_im, bblk, cblk = _ssm_weights(ssm_a_re, ssm_a_im, ssm_log_dt, ssm_b_re, ssm_b_im,
                                          ssm_c_re, ssm_c_im)
    w = dict(
        w_s=w_s.astype(F32), g_mix=row(g_mix), w_in=w_in.astype(BF16), conv_a_w=conv_a_w.astype(F32),
        conv_a_b=row(conv_a_b), ln_a_g=row(ln_a_g), ln_a_b=row(ln_a_b),
        w_a_out=w_a_out.astype(BF16), ln_b_g=row(ln_b_g), ln_b_b=row(ln_b_b),
        w_b_out=w_b_out.astype(BF16), a_re=a_re, a_im=a_im, bblk=bblk, cblk=cblk,
        ssm_d=row(ssm_d), w_c_glu=w_c_glu.astype(BF16), w_out=w_out.astype(BF16),
        g_ffn=row(g_ffn), w_up=w_up.astype(BF16), conv_f_w=conv_f_w.astype(F32),
        conv_f_b=row(conv_f_b), w_down=w_down.astype(BF16), g_ple=row(g_ple),
        w_pg=w_pg.astype(BF16), w_pe=w_pe.astype(BF16))
    gfin = g_final.reshape(1, -1).astype(F32)

    bp = x_prompt.shape[0]
    zero_a = jnp.zeros((depth, bp, HIST_A, D_A), F32)
    zero_s = jnp.zeros((depth, bp, N_GROUPS_C, SSM_STATE), F32)
    zero_f = jnp.zeros((depth, bp, HIST_F, 2 * D_FF), F32)
    y_p, ca_p, cv_p, re_p, im_p, cf_p = _trunk(
        x_prompt, p_prompt, zero_a, zero_s, zero_s, zero_f, w, b_s, gfin, 64)
    y_s, ca_s, cv_s, re_s, im_s, cf_s = _trunk(
        x_sample, p_sample, state_conv_a, state_ssm_re, state_ssm_im, state_conv_ffn,
        w, b_s, gfin, 1)
    return (y_p, y_s, ca_p, ca_s, cv_p, cv_s, re_p, im_p, re_s, im_s, cf_p, cf_s)
```

```python
import functools

import jax
import jax.numpy as jnp
from jax import lax
from jax.experimental import pallas as pl
from jax.experimental.pallas import tpu as pltpu

D_MODEL = 1024
D_PLE = 256
D_A = 512
D_B = 512
D_C = 512
CONV_A_WIDTH = 31
CHUNK = 128
N_GROUPS_B = 4
GW_B = D_B // N_GROUPS_B
SSM_GROUP = 16
N_GROUPS_C = D_C // SSM_GROUP
SSM_STATE = 64
D_FF = 2816
FFN_CONV_WIDTH = 3
EPS = 1e-6

N_STATE = N_GROUPS_C * SSM_STATE
SSM_SUB = 4
SUB_IN = D_C // SSM_SUB
SUB_ST = N_STATE // SSM_SUB
HIST_A = CONV_A_WIDTH - 1
HIST_F = FFN_CONV_WIDTH - 1
O_B = 2 * D_A
O_C = O_B + 2 * D_B
O_G = O_C + D_C
FF_CHUNKS = ((0, 1024), (1024, 1024), (2048, 768))
CONV_ROWS = 32
VMEM_LIMIT = 58 * 1024 * 1024

F32 = jnp.float32
BF16 = jnp.bfloat16


def _rms(x, g):
    ms = jnp.mean(x * x, axis=-1, keepdims=True)
    return x * lax.rsqrt(ms + EPS) * g


def _ln(x, g, b):
    mu = jnp.mean(x, axis=-1, keepdims=True)
    xc = x - mu
    var = jnp.mean(xc * xc, axis=-1, keepdims=True)
    return xc * lax.rsqrt(var + EPS) * g + b


def _dot(a, b):
    return jnp.dot(a, b, preferred_element_type=F32)


def _load_rows(ref, nb, tt):
    v = ref[...]
    if v.ndim == 3:
        v = jnp.swapaxes(v, 0, 1).reshape(tt * nb, v.shape[-1])
    return v


def _build_kron(ws_ref, kron_s, nb, tt, npos):
    rows = nb * tt
    shift = nb.bit_length() - 1
    assert nb == 1 << shift
    r_i = lax.broadcasted_iota(jnp.int32, (rows, rows), 0)
    c_i = lax.broadcasted_iota(jnp.int32, (rows, rows), 1)
    same_batch = (r_i & (nb - 1)) == (c_i & (nb - 1))
    blocks = [(0, 0)] if npos == 1 else [(0, 0), (1, 0), (1, 1)]
    if tt > 1:
        e_r = lax.shift_right_logical(lax.broadcasted_iota(jnp.int32, (rows, tt), 0), shift)
        expand = jnp.where(e_r == lax.broadcasted_iota(jnp.int32, (rows, tt), 1), 1.0, 0.0)
        e_c = lax.shift_right_logical(lax.broadcasted_iota(jnp.int32, (tt, rows), 1), shift)
        expand_t = jnp.where(e_c == lax.broadcasted_iota(jnp.int32, (tt, rows), 0), 1.0, 0.0)
        expand, expand_t = expand.astype(BF16), expand_t.astype(BF16)
        tril = (lax.broadcasted_iota(jnp.int32, (tt, tt), 1)
                <= lax.broadcasted_iota(jnp.int32, (tt, tt), 0))
    for slab, (p, q) in enumerate(blocks):
        for g in range(N_GROUPS_B):
            blk = ws_ref[g, p * tt:(p + 1) * tt, q * tt:(q + 1) * tt]
            if tt == 1:
                k = jnp.broadcast_to(blk, (rows, rows))
            else:
                if p == q:
                    blk = jnp.where(tril, blk, 0.0)
                k = _dot(_dot(expand, blk.T.astype(BF16)).astype(BF16), expand_t)
            kron_s[slab, g] = jnp.where(same_batch, k, 0.0).astype(BF16)


def _mixer_kernel(cfg, x_ref, hist_ref, s0re_ref, s0im_ref, bsb_ref, ws_ref, gmix_ref, win_ref,
                  caw_ref, cab_ref, lag_ref, lab_ref, waout_ref, lbg_ref, lbb_ref, wbout_ref,
                  are_ref, aim_ref, bblk_ref, cblk_ref, ssmd_ref, wcglu_ref, wout_ref,
                  x1_ref, nhist_ref, cv_ref, nsre_ref, nsim_ref,
                  hist_s, vprev_s, sst_s, kron_s):
    nb, tt, npos = cfg
    rows = nb * tt
    hb = HIST_A * nb
    i = pl.program_id(0)

    @pl.when(i == 0)
    def _init():
        hist_s[...] = hist_ref[...]
        sst_s[0] = s0re_ref[...]
        sst_s[1] = s0im_ref[...]
        vprev_s[...] = jnp.zeros(vprev_s.shape, vprev_s.dtype)
        _build_kron(ws_ref, kron_s, nb, tt, npos)

    x = _load_rows(x_ref, nb, tt)
    h = _rms(x, gmix_ref[...]).astype(BF16)

    za = _dot(h, win_ref[:, 0:O_B])
    zc = _dot(h, win_ref[:, O_C:O_G])
    bus = [_dot(zc[:, q * SUB_IN:(q + 1) * SUB_IN].astype(BF16), bblk_ref[q])
           for q in range(SSM_SUB)]
    zb = _dot(h, win_ref[:, O_B:O_C])

    def gate(j):
        return jax.nn.sigmoid(_dot(h, win_ref[:, O_G + j * D_MODEL:O_G + (j + 1) * D_MODEL]))

    u = zb[:, :D_B]
    v = _ln(zb[:, D_B:], lbg_ref[...], lbb_ref[...])
    cv_ref[...] = v
    vm_parts = []
    pos = i % 2
    for g in range(N_GROUPS_B):
        v_t = v[:, g * GW_B:(g + 1) * GW_B].T
        mixed_t = _dot(v_t.astype(BF16), kron_s[0 if npos == 1 else 2 * pos, g])
        if npos > 1:
            mixed_t = mixed_t + _dot(vprev_s[g].astype(BF16), kron_s[1, g])
            vprev_s[g] = jnp.where(pos == 0, v_t, 0.0)
        vm_parts.append(mixed_t.T)
    bias_rows = [jnp.broadcast_to(bsb_ref[0, t:t + 1, :], (nb, D_B)) for t in range(tt)]
    bias = jnp.concatenate(bias_rows, axis=0) if tt > 1 else bias_rows[0]
    vm = jnp.concatenate(vm_parts, axis=1) + bias
    uvm = (u * vm).astype(BF16)
    g_a = gate(0)

    a = za[:, :D_A] * jax.nn.sigmoid(za[:, D_A:])
    cat = jnp.concatenate([hist_s[...], a], axis=0)
    new_hist = cat[rows:rows + hb]
    hist_s[...] = new_hist
    nhist_ref[...] = new_hist
    cab = cab_ref[...]
    conv_blocks = []
    rb = min(CONV_ROWS, rows)
    for r0 in range(0, rows, rb):
        acc = jnp.broadcast_to(cab, (rb, D_A))
        for k in range(CONV_A_WIDTH):
            acc = acc + caw_ref[k:k + 1, :] * cat[k * nb + r0:k * nb + r0 + rb]
        conv_blocks.append(acc)
    a_conv = jnp.concatenate(conv_blocks, axis=0) if len(conv_blocks) > 1 else conv_blocks[0]
    a_act = _ln(a_conv, lag_ref[...], lab_ref[...])
    a_act = (a_act * jax.nn.sigmoid(a_act)).astype(BF16)
    m = g_a * _dot(a_act, waout_ref[...])
    m = m + gate(1) * _dot(uvm, wbout_ref[...])

    y_parts = []
    for q in range(SSM_SUB):
        uq = zc[:, q * SUB_IN:(q + 1) * SUB_IN]
        bu = bus[q]
        st = slice(q * SUB_ST, (q + 1) * SUB_ST)
        ar = jnp.broadcast_to(are_ref[:, st], (nb, SUB_ST))
        ai = jnp.broadcast_to(aim_ref[:, st], (nb, SUB_ST))
        sr = sst_s[0, :, st]
        si = sst_s[1, :, st]
        s_rows = []
        for t in range(tt):
            bt = bu[t * nb:(t + 1) * nb]
            nr = ar * sr - ai * si + bt[:, :SUB_ST]
            ni = ar * si + ai * sr + bt[:, SUB_ST:]
            sr, si = nr, ni
            s_rows.append(jnp.concatenate([nr, ni], axis=1))
        sst_s[0, :, st] = sr
        sst_s[1, :, st] = si
        nsre_ref[:, st] = sr
        nsim_ref[:, st] = si
        s_all = jnp.concatenate(s_rows, axis=0) if tt > 1 else s_rows[0]
        yq = _dot(s_all.astype(BF16), cblk_ref[q]) + ssmd_ref[:, q * SUB_IN:(q + 1) * SUB_IN] * uq
        y_parts.append(jax.nn.gelu(yq).astype(BF16))
    yc = jnp.concatenate(y_parts, axis=1)
    g_c = gate(2)
    gc = _dot(yc, wcglu_ref[...])
    m = m + g_c * (gc[:, :D_MODEL] * jax.nn.sigmoid(gc[:, D_MODEL:]))

    x1_ref[...] = x + _dot(m.astype(BF16), wout_ref[...])


def _ffn_kernel(cfg, x_ref, p_ref, hist_ref, gfin_ref, gffn_ref, wup_ref, cfw_ref, cfb_ref,
                wdown_ref, gple_ref, wpg_ref, wpe_ref,
                xo_ref, nhist_ref,
                hist_s):
    nb, tt, final = cfg
    rows = nb * tt
    hb = HIST_F * nb
    i = pl.program_id(0)

    @pl.when(i == 0)
    def _init():
        hist_s[...] = hist_ref[...]

    x = x_ref[...]
    p = _load_rows(p_ref, nb, tt).astype(BF16)
    h = _rms(x, gffn_ref[...]).astype(BF16)
    f = None
    for c0, cw in FF_CHUNKS:
        halves = []
        for off in (c0, D_FF + c0):
            up = _dot(h, wup_ref[:, off:off + cw])
            cat = jnp.concatenate([hist_s[:, off:off + cw], up], axis=0)
            new_hist = cat[rows:rows + hb]
            hist_s[:, off:off + cw] = new_hist
            nhist_ref[:, off:off + cw] = new_hist
            y = cfb_ref[:, off:off + cw] + cfw_ref[0:1, off:off + cw] * cat[0:rows]
            y = y + cfw_ref[1:2, off:off + cw] * cat[nb:nb + rows]
            y = y + cfw_ref[2:3, off:off + cw] * up
            halves.append(y)
        act = (jax.nn.gelu(halves[0]) * halves[1]).astype(BF16)
        t = _dot(act, wdown_ref[c0:c0 + cw, :])
        f = t if f is None else f + t
    x2 = x + f
    h3 = _rms(x2, gple_ref[...]).astype(BF16)
    gate = jax.nn.sigmoid(_dot(h3, wpg_ref[...]))
    x3 = x2 + gate * _dot(p, wpe_ref[...])
    if final:
        x3 = _rms(x3, gfin_ref[...])
    if len(xo_ref.shape) == 3:
        x3 = jnp.swapaxes(x3.reshape(tt, nb, D_MODEL), 0, 1)
    xo_ref[...] = x3


def _const_spec(shape):
    nd = len(shape)
    return pl.BlockSpec(shape, lambda i: (0,) * nd, pipeline_mode=pl.Buffered(1))


def _layer_spec(shape, layer):
    nd = len(shape)
    return pl.BlockSpec((None,) + tuple(shape[1:]), lambda i: (layer,) + (0,) * (nd - 1),
                        pipeline_mode=pl.Buffered(1))


def _pos_spec(shape, npos):
    nd = len(shape)
    return pl.BlockSpec((1,) + tuple(shape[1:]), lambda i: (i % npos,) + (0,) * (nd - 1))


def _out_const_spec(shape):
    nd = len(shape)
    return pl.BlockSpec(shape, lambda i: (0,) * nd)


MIXER_PARAMS = ('w_s', 'g_mix', 'w_in', 'conv_a_w', 'conv_a_b', 'ln_a_g', 'ln_a_b', 'w_a_out', 'ln_b_g',
                'ln_b_b', 'w_b_out', 'a_re', 'a_im', 'bblk', 'cblk', 'ssm_d', 'w_c_glu', 'w_out')
FFN_PARAMS = ('g_ffn', 'w_up', 'conv_f_w', 'conv_f_b', 'w_down', 'g_ple', 'w_pg', 'w_pe')


def _mixer_call(layer, nb, tt, nt, x, hist0, s0re, s0im, bsb, w):
    rows = nb * tt
    t_total = tt * nt
    npos = bsb.shape[0]
    last_start = ((t_total - 1) // CHUNK) * CHUNK
    n_last = (t_total - last_start) // tt
    stacked = [w[k] for k in MIXER_PARAMS]
    x_spec = (pl.BlockSpec((nb, tt, D_MODEL), lambda i: (0, i, 0)) if x.ndim == 3
              else pl.BlockSpec((rows, D_MODEL), lambda i: (i, 0)))
    in_specs = [x_spec,
                _const_spec(hist0.shape), _const_spec(s0re.shape), _const_spec(s0im.shape),
                _pos_spec(bsb.shape, npos)]
    in_specs += [_layer_spec(c.shape, layer) for c in stacked]
    out_shape = [
        jax.ShapeDtypeStruct((nt * rows, D_MODEL), F32),
        jax.ShapeDtypeStruct((HIST_A * nb, D_A), F32),
        jax.ShapeDtypeStruct((n_last * rows, D_B), F32),
        jax.ShapeDtypeStruct((nb, N_STATE), F32),
        jax.ShapeDtypeStruct((nb, N_STATE), F32),
    ]
    out_specs = [
        pl.BlockSpec((rows, D_MODEL), lambda i: (i, 0)),
        _out_const_spec((HIST_A * nb, D_A)),
        pl.BlockSpec((rows, D_B), lambda i: (jnp.maximum(i - (nt - n_last), 0), 0)),
        _out_const_spec((nb, N_STATE)),
        _out_const_spec((nb, N_STATE)),
    ]
    scratch = [
        pltpu.VMEM((HIST_A * nb, D_A), F32),
        pltpu.VMEM((N_GROUPS_B, GW_B, rows), F32),
        pltpu.VMEM((2, nb, N_STATE), F32),
        pltpu.VMEM((1 if npos == 1 else 3, N_GROUPS_B, rows, rows), BF16),
    ]
    return pl.pallas_call(
        functools.partial(_mixer_kernel, (nb, tt, npos)),
        grid=(nt,),
        in_specs=in_specs,
        out_specs=out_specs,
        out_shape=out_shape,
        scratch_shapes=scratch,
        compiler_params=pltpu.CompilerParams(
            dimension_semantics=("arbitrary",), vmem_limit_bytes=VMEM_LIMIT),
        name=f"mixer_b{nb}_t{tt}",
    )(x, hist0, s0re, s0im, bsb, *stacked)


def _ffn_call(layer, nb, tt, nt, final, out_bt, x, p, hist0, g_final, w):
    rows = nb * tt
    stacked = [w[k] for k in FFN_PARAMS]
    p_spec = (pl.BlockSpec((None, nb, tt, D_PLE), lambda i: (layer, 0, i, 0)) if p.ndim == 4
              else pl.BlockSpec((None, rows, D_PLE), lambda i: (layer, i, 0)))
    in_specs = [pl.BlockSpec((rows, D_MODEL), lambda i: (i, 0)), p_spec,
                _const_spec(hist0.shape), _const_spec(g_final.shape)]
    in_specs += [_layer_spec(c.shape, layer) for c in stacked]
    if out_bt:
        x_out = jax.ShapeDtypeStruct((nb, nt * tt, D_MODEL), F32)
        x_out_spec = pl.BlockSpec((nb, tt, D_MODEL), lambda i: (0, i, 0))
    else:
        x_out = jax.ShapeDtypeStruct((nt * rows, D_MODEL), F32)
        x_out_spec = pl.BlockSpec((rows, D_MODEL), lambda i: (i, 0))
    out_shape = [x_out, jax.ShapeDtypeStruct((HIST_F * nb, 2 * D_FF), F32)]
    out_specs = [x_out_spec, _out_const_spec((HIST_F * nb, 2 * D_FF))]
    return pl.pallas_call(
        functools.partial(_ffn_kernel, (nb, tt, final)),
        grid=(nt,),
        in_specs=in_specs,
        out_specs=out_specs,
        out_shape=out_shape,
        scratch_shapes=[pltpu.VMEM((HIST_F * nb, 2 * D_FF), F32)],
        compiler_params=pltpu.CompilerParams(
            dimension_semantics=("arbitrary",), vmem_limit_bytes=VMEM_LIMIT),
        name=f"ffn_b{nb}_t{tt}",
    )(x, p, hist0, g_final, *stacked)


def _mix_bias_rows(b_s, tt, npos):
    bsb = []
    for p in range(npos):
        bias = b_s[:, p * tt:(p + 1) * tt]
        bias = jnp.broadcast_to(bias.T[:, :, None], (tt, N_GROUPS_B, GW_B))
        bsb.append(bias.reshape(tt, D_B))
    return jnp.stack(bsb).astype(F32)


def _ssm_prep_kernel(are_ref, aim_ref, ldt_ref, bre_ref, bim_ref, cre_ref, cim_ref,
                     abr_ref, abi_ref, bblk_ref, cblk_ref):
    a_re, a_im = are_ref[...], aim_ref[...]
    dt = jnp.exp(ldt_ref[...])
    mag = jnp.exp(a_re * dt)
    ang = a_im * dt
    abr, abi = mag * jnp.cos(ang), mag * jnp.sin(ang)
    abr_ref[...] = abr
    abi_ref[...] = abi
    den = a_re * a_re + a_im * a_im
    nr = abr - 1.0
    cr = (nr * a_re + abi * a_im) / den
    ci = (abi * a_re - nr * a_im) / den
    for q in range(SSM_SUB):
        st = slice(q * SUB_ST, (q + 1) * SUB_ST)
        bre, bim = bre_ref[q], bim_ref[q]
        bblk_ref[q, :, 0:SUB_ST] = (cr[:, st] * bre - ci[:, st] * bim).astype(BF16)
        bblk_ref[q, :, SUB_ST:2 * SUB_ST] = (cr[:, st] * bim + ci[:, st] * bre).astype(BF16)
        cblk_ref[q, 0:SUB_ST, :] = cre_ref[q].astype(BF16)
        cblk_ref[q, SUB_ST:2 * SUB_ST, :] = (-cim_ref[q]).astype(BF16)


def _ssm_weights(a_re, a_im, log_dt, b_re, b_im, c_re, c_im):
    depth = a_re.shape[0]
    gl = N_GROUPS_C // SSM_SUB
    eye = jnp.eye(gl, dtype=F32)

    def pack_b(part):
        part = part.astype(F32).reshape(depth, SSM_SUB, gl, SSM_STATE, SSM_GROUP)
        part = jnp.swapaxes(part, 3, 4)[:, :, :, :, None, :] * eye[None, None, :, None, :, None]
        return part.reshape(depth, SSM_SUB, SUB_IN, SUB_ST)

    def pack_c(part):
        part = part.astype(F32).reshape(depth, SSM_SUB, gl, SSM_GROUP, SSM_STATE)
        part = jnp.swapaxes(part, 3, 4)[:, :, :, :, None, :] * eye[None, None, :, None, :, None]
        return part.reshape(depth, SSM_SUB, SUB_ST, SUB_IN)

    flat = lambda v: v.astype(F32).reshape(depth, 1, N_STATE)
    ldt = jnp.broadcast_to(log_dt.astype(F32)[:, :, None], (depth, N_GROUPS_C, SSM_STATE))
    ins = [flat(a_re), flat(a_im), flat(ldt), pack_b(b_re), pack_b(b_im), pack_c(c_re), pack_c(c_im)]
    spec = lambda shape: pl.BlockSpec((None,) + tuple(shape[1:]),
                                      lambda l: (l,) + (0,) * (len(shape) - 1))
    out_shape = [jax.ShapeDtypeStruct((depth, 1, N_STATE), F32),
                 jax.ShapeDtypeStruct((depth, 1, N_STATE), F32),
                 jax.ShapeDtypeStruct((depth, SSM_SUB, SUB_IN, 2 * SUB_ST), BF16),
                 jax.ShapeDtypeStruct((depth, SSM_SUB, 2 * SUB_ST, SUB_IN), BF16)]
    return pl.pallas_call(
        _ssm_prep_kernel,
        grid=(depth,),
        in_specs=[spec(v.shape) for v in ins],
        out_specs=[spec(o.shape) for o in out_shape],
        out_shape=out_shape,
        name="ssm_prep",
    )(*ins)


def _to_tb(x):
    *lead, b, t, c = x.shape
    return jnp.swapaxes(x, -3, -2).reshape(*lead, t * b, c)


def _from_tb(x, b):
    *lead, tb, c = x.shape
    return jnp.swapaxes(x.reshape(*lead, tb // b, b, c), -3, -2)


def _trunk(x, p, buf_a, s_re, s_im, buf_f, w, b_s, g_final, tt):
    nb, t_total, _ = x.shape
    nt = t_total // tt
    depth = p.shape[0]
    npos = min(max(CHUNK // tt, 1), nt)
    assert npos in (1, 2) and tt <= CHUNK and CHUNK % tt == 0
    if tt > 1:
        xr, pr = x, p
    else:
        xr, pr = x.reshape(nb, D_MODEL), p.reshape(depth, nb, D_PLE)
    hist_a = _to_tb(buf_a)
    hist_f = _to_tb(buf_f)
    s_re = s_re.reshape(depth, nb, N_STATE)
    s_im = s_im.reshape(depth, nb, N_STATE)
    la, lv, lre, lim, lf = [], [], [], [], []
    for i in range(depth):
        bsb = _mix_bias_rows(b_s[i], tt, npos)
        xr, nh_a, cv, nre, nim = _mixer_call(i, nb, tt, nt, xr, hist_a[i], s_re[i], s_im[i],
                                             bsb, w)
        last = i == depth - 1
        xr, nh_f = _ffn_call(i, nb, tt, nt, last, last and tt > 1, xr, pr, hist_f[i], g_final, w)
        la.append(nh_a)
        lv.append(cv)
        lre.append(nre)
        lim.append(nim)
        lf.append(nh_f)
    state = lambda parts: jnp.stack(parts).reshape(depth, nb, N_GROUPS_C, SSM_STATE)
    return (xr.reshape(nb, t_total, D_MODEL), _from_tb(jnp.stack(la), nb),
            _from_tb(jnp.stack(lv), nb),
            state(lre), state(lim), _from_tb(jnp.stack(lf), nb))


def kernel(x_prompt, x_sample, state_conv_a, state_ssm_re, state_ssm_im, state_conv_ffn,
           p_prompt, p_sample, g_mix, w_in, conv_a_w, conv_a_b, ln_a_g, ln_a_b, w_a_out,
           ln_b_g, ln_b_b, w_s, b_s, w_b_out, ssm_a_re, ssm_a_im, ssm_log_dt, ssm_b_re,
           ssm_b_im, ssm_c_re, ssm_c_im, ssm_d, w_c_glu, w_out, g_ffn, w_up, conv_f_w,
           conv_f_b, w_down, g_ple, w_pg, w_pe, g_final):
    depth = w_in.shape[0]
    row = lambda a: a.reshape(depth, 1, -1).astype(F32)
    a_re, a_im, bblk, cblk = _ssm_weights(ssm_a_re, ssm_a_im, ssm_log_dt, ssm_b_re, ssm_b_im,
                                          ssm_c_re, ssm_c_im)
    w = dict(
        w_s=w_s.astype(F32), g_mix=row(g_mix), w_in=w_in.astype(BF16), conv_a_w=conv_a_w.astype(F32),
        conv_a_b=row(conv_a_b), ln_a_g=row(ln_a_g), ln_a_b=row(ln_a_b),
        w_a_out=w_a_out.astype(BF16), ln_b_g=row(ln_b_g), ln_b_b=row(ln_b_b),
        w_b_out=w_b_out.astype(BF16), a_re=a_re, a_im=a_im, bblk=bblk, cblk=cblk,
        ssm_d=row(ssm_d), w_c_glu=w_c_glu.astype(BF16), w_out=w_out.astype(BF16),
        g_ffn=row(g_ffn), w_up=w_up.astype(BF16), conv_f_w=conv_f_w.astype(F32),
        conv_f_b=row(conv_f_b), w_down=w_down.astype(BF16), g_ple=row(g_ple),
        w_pg=w_pg.astype(BF16), w_pe=w_pe.astype(BF16))
    gfin = g_final.reshape(1, -1).astype(F32)

    bp = x_prompt.shape[0]
    zero_a = jnp.zeros((depth, bp, HIST_A, D_A), F32)
    zero_s = jnp.zeros((depth, bp, N_GROUPS_C, SSM_STATE), F32)
    zero_f = jnp.zeros((depth, bp, HIST_F, 2 * D_FF), F32)
    y_p, ca_p, cv_p, re_p, im_p, cf_p = _trunk(
        x_prompt, p_prompt, zero_a, zero_s, zero_s, zero_f, w, b_s, gfin, 64)
    y_s, ca_s, cv_s, re_s, im_s, cf_s = _trunk(
        x_sample, p_sample, state_conv_a, state_ssm_re, state_ssm_im, state_conv_ffn,
        w, b_s, gfin, 1)
    return (y_p, y_s, ca_p, ca_s, cv_p, cv_s, re_p, im_p, re_s, im_s, cf_p, cf_s)
```

```python
import functools

import jax
import jax.numpy as jnp
from jax import lax
from jax.experimental import pallas as pl
from jax.experimental.pallas import tpu as pltpu

D_MODEL = 1024
D_PLE = 256
D_A = 512
D_B = 512
D_C = 512
CONV_A_WIDTH = 31
CHUNK = 128
N_GROUPS_B = 4
GW_B = D_B // N_GROUPS_B
SSM_GROUP = 16
N_GROUPS_C = D_C // SSM_GROUP
SSM_STATE = 64
D_FF = 2816
FFN_CONV_WIDTH = 3
EPS = 1e-6

N_STATE = N_GROUPS_C * SSM_STATE
SSM_SUB = 4
SUB_IN = D_C // SSM_SUB
SUB_ST = N_STATE // SSM_SUB
HIST_A = CONV_A_WIDTH - 1
HIST_F = FFN_CONV_WIDTH - 1
O_B = 2 * D_A
O_C = O_B + 2 * D_B
O_G = O_C + D_C
FF_CHUNKS = ((0, 1024), (1024, 1024), (2048, 768))
CONV_ROWS = 32
VMEM_LIMIT = 58 * 1024 * 1024
TT_MIXER = 64
TT_FFN = 64

F32 = jnp.float32
BF16 = jnp.bfloat16


def _rms(x, g):
    ms = jnp.mean(x * x, axis=-1, keepdims=True)
    return x * lax.rsqrt(ms + EPS) * g


def _ln(x, g, b):
    mu = jnp.mean(x, axis=-1, keepdims=True)
    xc = x - mu
    var = jnp.mean(xc * xc, axis=-1, keepdims=True)
    return xc * lax.rsqrt(var + EPS) * g + b


def _dot(a, b):
    return jnp.dot(a, b, preferred_element_type=F32)


def _load_rows(ref, nb, tt):
    v = ref[...]
    if v.ndim == 3:
        v = jnp.swapaxes(v, 0, 1).reshape(tt * nb, v.shape[-1])
    return v


def _build_kron(ws_ref, kron_s, nb, tt, npos):
    rows = nb * tt
    shift = nb.bit_length() - 1
    assert nb == 1 << shift
    r_i = lax.broadcasted_iota(jnp.int32, (rows, rows), 0)
    c_i = lax.broadcasted_iota(jnp.int32, (rows, rows), 1)
    same_batch = (r_i & (nb - 1)) == (c_i & (nb - 1))
    blocks = [(0, 0)] if npos == 1 else [(0, 0), (1, 0), (1, 1)]
    if tt > 1:
        e_r = lax.shift_right_logical(lax.broadcasted_iota(jnp.int32, (rows, tt), 0), shift)
        expand = jnp.where(e_r == lax.broadcasted_iota(jnp.int32, (rows, tt), 1), 1.0, 0.0)
        e_c = lax.shift_right_logical(lax.broadcasted_iota(jnp.int32, (tt, rows), 1), shift)
        expand_t = jnp.where(e_c == lax.broadcasted_iota(jnp.int32, (tt, rows), 0), 1.0, 0.0)
        expand, expand_t = expand.astype(BF16), expand_t.astype(BF16)
        tril = (lax.broadcasted_iota(jnp.int32, (tt, tt), 1)
                <= lax.broadcasted_iota(jnp.int32, (tt, tt), 0))
    for slab, (p, q) in enumerate(blocks):
        for g in range(N_GROUPS_B):
            blk = ws_ref[g, p * tt:(p + 1) * tt, q * tt:(q + 1) * tt]
            if tt == 1:
                k = jnp.broadcast_to(blk, (rows, rows))
            else:
                if p == q:
                    blk = jnp.where(tril, blk, 0.0)
                k = _dot(_dot(expand, blk.T.astype(BF16)).astype(BF16), expand_t)
            kron_s[slab, g] = jnp.where(same_batch, k, 0.0).astype(BF16)


def _mixer_kernel(cfg, x_ref, hist_ref, s0re_ref, s0im_ref, bsb_ref, ws_ref, gmix_ref, win_ref,
                  caw_ref, cab_ref, lag_ref, lab_ref, waout_ref, lbg_ref, lbb_ref, wbout_ref,
                  are_ref, aim_ref, bblk_ref, cblk_ref, ssmd_ref, wcglu_ref, wout_ref,
                  x1_ref, nhist_ref, cv_ref, nsre_ref, nsim_ref,
                  hist_s, vprev_s, sst_s, kron_s):
    nb, tt, npos = cfg
    rows = nb * tt
    hb = HIST_A * nb
    i = pl.program_id(0)

    @pl.when(i == 0)
    def _init():
        hist_s[...] = _load_rows(hist_ref, nb, HIST_A)
        sst_s[0] = s0re_ref[...]
        sst_s[1] = s0im_ref[...]
        vprev_s[...] = jnp.zeros(vprev_s.shape, vprev_s.dtype)
        _build_kron(ws_ref, kron_s, nb, tt, npos)

    x = _load_rows(x_ref, nb, tt)
    h = _rms(x, gmix_ref[...]).astype(BF16)

    za = _dot(h, win_ref[:, 0:O_B])
    zc = _dot(h, win_ref[:, O_C:O_G])
    bus = [_dot(zc[:, q * SUB_IN:(q + 1) * SUB_IN].astype(BF16), bblk_ref[q])
           for q in range(SSM_SUB)]
    zb = _dot(h, win_ref[:, O_B:O_C])

    def gate(j):
        return jax.nn.sigmoid(_dot(h, win_ref[:, O_G + j * D_MODEL:O_G + (j + 1) * D_MODEL]))

    u = zb[:, :D_B]
    v = _ln(zb[:, D_B:], lbg_ref[...], lbb_ref[...])
    cv_ref[...] = v
    vm_parts = []
    pos = i % 2
    for g in range(N_GROUPS_B):
        v_t = v[:, g * GW_B:(g + 1) * GW_B].T
        mixed_t = _dot(v_t.astype(BF16), kron_s[0 if npos == 1 else 2 * pos, g])
        if npos > 1:
            mixed_t = mixed_t + _dot(vprev_s[g].astype(BF16), kron_s[1, g])
            vprev_s[g] = jnp.where(pos == 0, v_t, 0.0)
        vm_parts.append(mixed_t.T)
    bias_rows = [jnp.broadcast_to(bsb_ref[0, t:t + 1, :], (nb, D_B)) for t in range(tt)]
    bias = jnp.concatenate(bias_rows, axis=0) if tt > 1 else bias_rows[0]
    vm = jnp.concatenate(vm_parts, axis=1) + bias
    uvm = (u * vm).astype(BF16)
    g_a = gate(0)

    a = za[:, :D_A] * jax.nn.sigmoid(za[:, D_A:])
    cat = jnp.concatenate([hist_s[...], a], axis=0)
    new_hist = cat[rows:rows + hb]
    hist_s[...] = new_hist
    nhist_ref[...] = new_hist
    cab = cab_ref[...]
    conv_blocks = []
    rb = min(CONV_ROWS, rows)
    for r0 in range(0, rows, rb):
        acc = jnp.broadcast_to(cab, (rb, D_A))
        for k in range(CONV_A_WIDTH):
            acc = acc + caw_ref[k:k + 1, :] * cat[k * nb + r0:k * nb + r0 + rb]
        conv_blocks.append(acc)
    a_conv = jnp.concatenate(conv_blocks, axis=0) if len(conv_blocks) > 1 else conv_blocks[0]
    a_act = _ln(a_conv, lag_ref[...], lab_ref[...])
    a_act = (a_act * jax.nn.sigmoid(a_act)).astype(BF16)
    m = g_a * _dot(a_act, waout_ref[...])
    m = m + gate(1) * _dot(uvm, wbout_ref[...])

    y_parts = []
    for q in range(SSM_SUB):
        uq = zc[:, q * SUB_IN:(q + 1) * SUB_IN]
        bu = bus[q]
        st = slice(q * SUB_ST, (q + 1) * SUB_ST)
        ar = jnp.broadcast_to(are_ref[:, st], (nb, SUB_ST))
        ai = jnp.broadcast_to(aim_ref[:, st], (nb, SUB_ST))
        sr = sst_s[0, :, st]
        si = sst_s[1, :, st]
        s_rows = []
        for t in range(tt):
            bt = bu[t * nb:(t + 1) * nb]
            nr = ar * sr - ai * si + bt[:, :SUB_ST]
            ni = ar * si + ai * sr + bt[:, SUB_ST:]
            sr, si = nr, ni
            s_rows.append(jnp.concatenate([nr, ni], axis=1))
        sst_s[0, :, st] = sr
        sst_s[1, :, st] = si
        nsre_ref[:, st] = sr
        nsim_ref[:, st] = si
        s_all = jnp.concatenate(s_rows, axis=0) if tt > 1 else s_rows[0]
        yq = _dot(s_all.astype(BF16), cblk_ref[q]) + ssmd_ref[:, q * SUB_IN:(q + 1) * SUB_IN] * uq
        y_parts.append(jax.nn.gelu(yq).astype(BF16))
    yc = jnp.concatenate(y_parts, axis=1)
    g_c = gate(2)
    gc = _dot(yc, wcglu_ref[...])
    m = m + g_c * (gc[:, :D_MODEL] * jax.nn.sigmoid(gc[:, D_MODEL:]))

    x1_ref[...] = x + _dot(m.astype(BF16), wout_ref[...])


def _ffn_kernel(cfg, x_ref, p_ref, hist_ref, gfin_ref, gffn_ref, wup_ref, cfw_ref, cfb_ref,
                wdown_ref, gple_ref, wpg_ref, wpe_ref,
                xo_ref, nhist_ref,
                hist_s):
    nb, tt, final = cfg
    rows = nb * tt
    hb = HIST_F * nb
    i = pl.program_id(0)

    @pl.when(i == 0)
    def _init():
        hist_s[...] = _load_rows(hist_ref, nb, HIST_F)

    x = x_ref[...]
    p = _load_rows(p_ref, nb, tt).astype(BF16)
    h = _rms(x, gffn_ref[...]).astype(BF16)
    f = None
    for c0, cw in FF_CHUNKS:
        halves = []
        for off in (c0, D_FF + c0):
            up = _dot(h, wup_ref[:, off:off + cw])
            cat = jnp.concatenate([hist_s[:, off:off + cw], up], axis=0)
            new_hist = cat[rows:rows + hb]
            hist_s[:, off:off + cw] = new_hist
            nhist_ref[:, off:off + cw] = new_hist
            y = cfb_ref[:, off:off + cw] + cfw_ref[0:1, off:off + cw] * cat[0:rows]
            y = y + cfw_ref[1:2, off:off + cw] * cat[nb:nb + rows]
            y = y + cfw_ref[2:3, off:off + cw] * up
            halves.append(y)
        act = (jax.nn.gelu(halves[0]) * halves[1]).astype(BF16)
        t = _dot(act, wdown_ref[c0:c0 + cw, :])
        f = t if f is None else f + t
    x2 = x + f
    h3 = _rms(x2, gple_ref[...]).astype(BF16)
    gate = jax.nn.sigmoid(_dot(h3, wpg_ref[...]))
    x3 = x2 + gate * _dot(p, wpe_ref[...])
    if final:
        x3 = _rms(x3, gfin_ref[...])
    if len(xo_ref.shape) == 3:
        x3 = jnp.swapaxes(x3.reshape(tt, nb, D_MODEL), 0, 1)
    xo_ref[...] = x3


def _const_spec(shape):
    nd = len(shape)
    return pl.BlockSpec(shape, lambda i: (0,) * nd, pipeline_mode=pl.Buffered(1))


def _layer_spec(shape, layer):
    nd = len(shape)
    return pl.BlockSpec((None,) + tuple(shape[1:]), lambda i: (layer,) + (0,) * (nd - 1),
                        pipeline_mode=pl.Buffered(1))


def _pos_spec(shape, npos):
    nd = len(shape)
    return pl.BlockSpec((1,) + tuple(shape[1:]), lambda i: (i % npos,) + (0,) * (nd - 1))


def _out_const_spec(shape):
    nd = len(shape)
    return pl.BlockSpec(shape, lambda i: (0,) * nd)


MIXER_PARAMS = ('w_s', 'g_mix', 'w_in', 'conv_a_w', 'conv_a_b', 'ln_a_g', 'ln_a_b', 'w_a_out', 'ln_b_g',
                'ln_b_b', 'w_b_out', 'a_re', 'a_im', 'bblk', 'cblk', 'ssm_d', 'w_c_glu', 'w_out')
FFN_PARAMS = ('g_ffn', 'w_up', 'conv_f_w', 'conv_f_b', 'w_down', 'g_ple', 'w_pg', 'w_pe')


def _mixer_call(layer, nb, tt, nt, x, hist0, s0re, s0im, bsb, w):
    rows = nb * tt
    t_total = tt * nt
    npos = bsb.shape[0]
    last_start = ((t_total - 1) // CHUNK) * CHUNK
    n_last = (t_total - last_start) // tt
    stacked = [w[k] for k in MIXER_PARAMS]
    x_spec = (pl.BlockSpec((nb, tt, D_MODEL), lambda i: (0, i, 0)) if x.ndim == 3
              else pl.BlockSpec((rows, D_MODEL), lambda i: (i, 0)))
    in_specs = [x_spec,
                _layer_spec(hist0.shape, layer), _layer_spec(s0re.shape, layer),
                _layer_spec(s0im.shape, layer),
                _pos_spec(bsb.shape, npos)]
    in_specs += [_layer_spec(c.shape, layer) for c in stacked]
    out_shape = [
        jax.ShapeDtypeStruct((nt * rows, D_MODEL), F32),
        jax.ShapeDtypeStruct((HIST_A * nb, D_A), F32),
        jax.ShapeDtypeStruct((n_last * rows, D_B), F32),
        jax.ShapeDtypeStruct((nb, N_STATE), F32),
        jax.ShapeDtypeStruct((nb, N_STATE), F32),
    ]
    out_specs = [
        pl.BlockSpec((rows, D_MODEL), lambda i: (i, 0)),
        _out_const_spec((HIST_A * nb, D_A)),
        pl.BlockSpec((rows, D_B), lambda i: (jnp.maximum(i - (nt - n_last), 0), 0)),
        _out_const_spec((nb, N_STATE)),
        _out_const_spec((nb, N_STATE)),
    ]
    scratch = [
        pltpu.VMEM((HIST_A * nb, D_A), F32),
        pltpu.VMEM((N_GROUPS_B, GW_B, rows), F32),
        pltpu.VMEM((2, nb, N_STATE), F32),
        pltpu.VMEM((1 if npos == 1 else 3, N_GROUPS_B, rows, rows), BF16),
    ]
    return pl.pallas_call(
        functools.partial(_mixer_kernel, (nb, tt, npos)),
        grid=(nt,),
        in_specs=in_specs,
        out_specs=out_specs,
        out_shape=out_shape,
        scratch_shapes=scratch,
        compiler_params=pltpu.CompilerParams(
            dimension_semantics=("arbitrary",), vmem_limit_bytes=VMEM_LIMIT),
        name=f"mixer_b{nb}_t{tt}",
    )(x, hist0, s0re, s0im, bsb, *stacked)


def _ffn_call(layer, nb, tt, nt, final, out_bt, x, p, hist0, g_final, w):
    rows = nb * tt
    stacked = [w[k] for k in FFN_PARAMS]
    p_spec = (pl.BlockSpec((None, nb, tt, D_PLE), lambda i: (layer, 0, i, 0)) if p.ndim == 4
              else pl.BlockSpec((None, rows, D_PLE), lambda i: (layer, i, 0)))
    in_specs = [pl.BlockSpec((rows, D_MODEL), lambda i: (i, 0)), p_spec,
                _layer_spec(hist0.shape, layer), _const_spec(g_final.shape)]
    in_specs += [_layer_spec(c.shape, layer) for c in stacked]
    if out_bt:
        x_out = jax.ShapeDtypeStruct((nb, nt * tt, D_MODEL), F32)
        x_out_spec = pl.BlockSpec((nb, tt, D_MODEL), lambda i: (0, i, 0))
    else:
        x_out = jax.ShapeDtypeStruct((nt * rows, D_MODEL), F32)
        x_out_spec = pl.BlockSpec((rows, D_MODEL), lambda i: (i, 0))
    out_shape = [x_out, jax.ShapeDtypeStruct((HIST_F * nb, 2 * D_FF), F32)]
    out_specs = [x_out_spec, _out_const_spec((HIST_F * nb, 2 * D_FF))]
    return pl.pallas_call(
        functools.partial(_ffn_kernel, (nb, tt, final)),
        grid=(nt,),
        in_specs=in_specs,
        out_specs=out_specs,
        out_shape=out_shape,
        scratch_shapes=[pltpu.VMEM((HIST_F * nb, 2 * D_FF), F32)],
        compiler_params=pltpu.CompilerParams(
            dimension_semantics=("arbitrary",), vmem_limit_bytes=VMEM_LIMIT),
        name=f"ffn_b{nb}_t{tt}",
    )(x, p, hist0, g_final, *stacked)


def _mix_bias_rows(b_s, tt, npos):
    bsb = []
    for p in range(npos):
        bias = b_s[:, p * tt:(p + 1) * tt]
        bias = jnp.broadcast_to(bias.T[:, :, None], (tt, N_GROUPS_B, GW_B))
        bsb.append(bias.reshape(tt, D_B))
    return jnp.stack(bsb).astype(F32)


def _ssm_prep_kernel(are_ref, aim_ref, ldt_ref, bre_ref, bim_ref, cre_ref, cim_ref,
                     abr_ref, abi_ref, bblk_ref, cblk_ref):
    a_re, a_im = are_ref[...], aim_ref[...]
    dt = jnp.exp(ldt_ref[...])
    mag = jnp.exp(a_re * dt)
    ang = a_im * dt
    abr, abi = mag * jnp.cos(ang), mag * jnp.sin(ang)
    abr_ref[...] = abr
    abi_ref[...] = abi
    nr = abr - 1.0
    re_big = jnp.abs(a_re) >= jnp.abs(a_im)
    ratio = jnp.where(re_big, a_im, a_re) / jnp.where(re_big, a_re, a_im)
    den = jnp.where(re_big, a_re + a_im * ratio, a_re * ratio + a_im)
    cr = jnp.where(re_big, nr + abi * ratio, nr * ratio + abi) / den
    ci = jnp.where(re_big, abi - nr * ratio, abi * ratio - nr) / den
    for q in range(SSM_SUB):
        st = slice(q * SUB_ST, (q + 1) * SUB_ST)
        bre, bim = bre_ref[q], bim_ref[q]
        bblk_ref[q, :, 0:SUB_ST] = (cr[:, st] * bre - ci[:, st] * bim).astype(BF16)
        bblk_ref[q, :, SUB_ST:2 * SUB_ST] = (cr[:, st] * bim + ci[:, st] * bre).astype(BF16)
        cblk_ref[q, 0:SUB_ST, :] = cre_ref[q].astype(BF16)
        cblk_ref[q, SUB_ST:2 * SUB_ST, :] = (-cim_ref[q]).astype(BF16)


def _ssm_weights(a_re, a_im, log_dt, b_re, b_im, c_re, c_im):
    depth = a_re.shape[0]
    gl = N_GROUPS_C // SSM_SUB
    eye = jnp.eye(gl, dtype=F32)

    def pack_b(part):
        part = part.astype(F32).reshape(depth, SSM_SUB, gl, SSM_STATE, SSM_GROUP)
        part = jnp.swapaxes(part, 3, 4)[:, :, :, :, None, :] * eye[None, None, :, None, :, None]
        return part.reshape(depth, SSM_SUB, SUB_IN, SUB_ST)

    def pack_c(part):
        part = part.astype(F32).reshape(depth, SSM_SUB, gl, SSM_GROUP, SSM_STATE)
        part = jnp.swapaxes(part, 3, 4)[:, :, :, :, None, :] * eye[None, None, :, None, :, None]
        return part.reshape(depth, SSM_SUB, SUB_ST, SUB_IN)

    flat = lambda v: v.astype(F32).reshape(depth, 1, N_STATE)
    ldt = jnp.broadcast_to(log_dt.astype(F32)[:, :, None], (depth, N_GROUPS_C, SSM_STATE))
    ins = [flat(a_re), flat(a_im), flat(ldt), pack_b(b_re), pack_b(b_im), pack_c(c_re), pack_c(c_im)]
    spec = lambda shape: pl.BlockSpec((None,) + tuple(shape[1:]),
                                      lambda l: (l,) + (0,) * (len(shape) - 1))
    out_shape = [jax.ShapeDtypeStruct((depth, 1, N_STATE), F32),
                 jax.ShapeDtypeStruct((depth, 1, N_STATE), F32),
                 jax.ShapeDtypeStruct((depth, SSM_SUB, SUB_IN, 2 * SUB_ST), BF16),
                 jax.ShapeDtypeStruct((depth, SSM_SUB, 2 * SUB_ST, SUB_IN), BF16)]
    return pl.pallas_call(
        _ssm_prep_kernel,
        grid=(depth,),
        in_specs=[spec(v.shape) for v in ins],
        out_specs=[spec(o.shape) for o in out_shape],
        out_shape=out_shape,
        name="ssm_prep",
    )(*ins)


def _to_tb(x):
    *lead, b, t, c = x.shape
    return jnp.swapaxes(x, -3, -2).reshape(*lead, t * b, c)


def _from_tb(x, b):
    *lead, tb, c = x.shape
    return jnp.swapaxes(x.reshape(*lead, tb // b, b, c), -3, -2)


def _trunk(x, p, buf_a, s_re, s_im, buf_f, w, b_s, g_final, tt, tt_f):
    nb, t_total, _ = x.shape
    nt = t_total // tt
    depth = p.shape[0]
    npos = min(max(CHUNK // tt, 1), nt)
    assert npos in (1, 2) and tt <= CHUNK and CHUNK % tt == 0
    if tt > 1:
        xr, pr = x, p
    else:
        xr, pr = x.reshape(nb, D_MODEL), p.reshape(depth, nb, D_PLE)
    hist_a = _to_tb(buf_a)
    hist_f = buf_f
    s_re = s_re.reshape(depth, nb, N_STATE)
    s_im = s_im.reshape(depth, nb, N_STATE)
    la, lv, lre, lim, lf = [], [], [], [], []
    for i in range(depth):
        bsb = _mix_bias_rows(b_s[i], tt, npos)
        xr, nh_a, cv, nre, nim = _mixer_call(i, nb, tt, nt, xr, hist_a, s_re, s_im,
                                             bsb, w)
        last = i == depth - 1
        xr, nh_f = _ffn_call(i, nb, tt_f, t_total // tt_f, last, last and tt > 1, xr, pr, hist_f,
                             g_final, w)
        la.append(nh_a)
        lv.append(cv)
        lre.append(nre)
        lim.append(nim)
        lf.append(nh_f)
    state = lambda parts: jnp.stack(parts).reshape(depth, nb, N_GROUPS_C, SSM_STATE)
    return (xr.reshape(nb, t_total, D_MODEL), _from_tb(jnp.stack(la), nb),
            _from_tb(jnp.stack(lv), nb),
            state(lre), state(lim), _from_tb(jnp.stack(lf), nb))


def kernel(x_prompt, x_sample, state_conv_a, state_ssm_re, state_ssm_im, state_conv_ffn,
           p_prompt, p_sample, g_mix, w_in, conv_a_w, conv_a_b, ln_a_g, ln_a_b, w_a_out,
           ln_b_g, ln_b_b, w_s, b_s, w_b_out, ssm_a_re, ssm_a_im, ssm_log_dt, ssm_b_re,
           ssm_b_im, ssm_c_re, ssm_c_im, ssm_d, w_c_glu, w_out, g_ffn, w_up, conv_f_w,
           conv_f_b, w_down, g_ple, w_pg, w_pe, g_final):
    depth = w_in.shape[0]
    row = lambda a: a.reshape(depth, 1, -1).astype(F32)
    a_re, a_im, bblk, cblk = _ssm_weights(ssm_a_re, ssm_a_im, ssm_log_dt, ssm_b_re, ssm_b_im,
                                          ssm_c_re, ssm_c_im)
    w = dict(
        w_s=w_s.astype(F32), g_mix=row(g_mix), w_in=w_in.astype(BF16), conv_a_w=conv_a_w.astype(F32),
        conv_a_b=row(conv_a_b), ln_a_g=row(ln_a_g), ln_a_b=row(ln_a_b),
        w_a_out=w_a_out.astype(BF16), ln_b_g=row(ln_b_g), ln_b_b=row(ln_b_b),
        w_b_out=w_b_out.astype(BF16), a_re=a_re, a_im=a_im, bblk=bblk, cblk=cblk,
        ssm_d=row(ssm_d), w_c_glu=w_c_glu.astype(BF16), w_out=w_out.astype(BF16),
        g_ffn=row(g_ffn), w_up=w_up.astype(BF16), conv_f_w=conv_f_w.astype(F32),
        conv_f_b=row(conv_f_b), w_down=w_down.astype(BF16), g_ple=row(g_ple),
        w_pg=w_pg.astype(BF16), w_pe=w_pe.astype(BF16))
    gfin = g_final.reshape(1, -1).astype(F32)

    bp = x_prompt.shape[0]
    zero_a = jnp.zeros((depth, bp, HIST_A, D_A), F32)
    zero_s = jnp.zeros((depth, bp, N_GROUPS_C, SSM_STATE), F32)
    zero_f = jnp.zeros((depth, bp, HIST_F, 2 * D_FF), F32)
    y_p, ca_p, cv_p, re_p, im_p, cf_p = _trunk(
        x_prompt, p_prompt, zero_a, zero_s, zero_s, zero_f, w, b_s, gfin, TT_MIXER, TT_FFN)
    y_s, ca_s, cv_s, re_s, im_s, cf_s = _trunk(
        x_sample, p_sample, state_conv_a, state_ssm_re, state_ssm_im, state_conv_ffn,
        w, b_s, gfin, 1, 1)
    return (y_p, y_s, ca_p, ca_s, cv_p, cv_s, re_p, im_p, re_s, im_s, cf_p, cf_s)
```

```python
import functools

import jax
import jax.numpy as jnp
from jax import lax
from jax.experimental import pallas as pl
from jax.experimental.pallas import tpu as pltpu

D_MODEL = 1024
D_PLE = 256
D_A = 512
D_B = 512
D_C = 512
CONV_A_WIDTH = 31
CHUNK = 128
N_GROUPS_B = 4
GW_B = D_B // N_GROUPS_B
SSM_GROUP = 16
N_GROUPS_C = D_C // SSM_GROUP
SSM_STATE = 64
D_FF = 2816
FFN_CONV_WIDTH = 3
EPS = 1e-6

N_STATE = N_GROUPS_C * SSM_STATE
SSM_SUB = 4
SUB_IN = D_C // SSM_SUB
SUB_ST = N_STATE // SSM_SUB
HIST_A = CONV_A_WIDTH - 1
HIST_F = FFN_CONV_WIDTH - 1
O_B = 2 * D_A
O_C = O_B + 2 * D_B
O_G = O_C + D_C
FF_CHUNKS = ((0, 1536), (1536, 1280))
CONV_ROWS = 32
VMEM_LIMIT = 58 * 1024 * 1024
TT_MIXER = 64
TT_FFN = 64

F32 = jnp.float32
BF16 = jnp.bfloat16


def _rms(x, g):
    ms = jnp.mean(x * x, axis=-1, keepdims=True)
    return x * lax.rsqrt(ms + EPS) * g


def _ln(x, g, b):
    mu = jnp.mean(x, axis=-1, keepdims=True)
    xc = x - mu
    var = jnp.mean(xc * xc, axis=-1, keepdims=True)
    return xc * lax.rsqrt(var + EPS) * g + b


def _dot(a, b):
    return jnp.dot(a, b, preferred_element_type=F32)


def _load_rows(ref, nb, tt):
    v = ref[...]
    if v.ndim == 3:
        v = jnp.swapaxes(v, 0, 1).reshape(tt * nb, v.shape[-1])
    return v


def _build_kron(ws_ref, kron_s, nb, tt, npos):
    rows = nb * tt
    shift = nb.bit_length() - 1
    assert nb == 1 << shift
    r_i = lax.broadcasted_iota(jnp.int32, (rows, rows), 0)
    c_i = lax.broadcasted_iota(jnp.int32, (rows, rows), 1)
    same_batch = (r_i & (nb - 1)) == (c_i & (nb - 1))
    blocks = [(0, 0)] if npos == 1 else [(0, 0), (1, 0), (1, 1)]
    if tt > 1:
        e_r = lax.shift_right_logical(lax.broadcasted_iota(jnp.int32, (rows, tt), 0), shift)
        expand = jnp.where(e_r == lax.broadcasted_iota(jnp.int32, (rows, tt), 1), 1.0, 0.0)
        e_c = lax.shift_right_logical(lax.broadcasted_iota(jnp.int32, (tt, rows), 1), shift)
        expand_t = jnp.where(e_c == lax.broadcasted_iota(jnp.int32, (tt, rows), 0), 1.0, 0.0)
        expand, expand_t = expand.astype(BF16), expand_t.astype(BF16)
        tril = (lax.broadcasted_iota(jnp.int32, (tt, tt), 1)
                <= lax.broadcasted_iota(jnp.int32, (tt, tt), 0))
    for slab, (p, q) in enumerate(blocks):
        for g in range(N_GROUPS_B):
            blk = ws_ref[g, p * tt:(p + 1) * tt, q * tt:(q + 1) * tt]
            if tt == 1:
                k = jnp.broadcast_to(blk, (rows, rows))
            else:
                if p == q:
                    blk = jnp.where(tril, blk, 0.0)
                k = _dot(_dot(expand, blk.T.astype(BF16)).astype(BF16), expand_t)
            kron_s[slab, g] = jnp.where(same_batch, k, 0.0).astype(BF16)


def _mixer_kernel(cfg, x_ref, hist_ref, s0re_ref, s0im_ref, bsb_ref, ws_ref, gmix_ref, win_ref,
                  caw_ref, cab_ref, lag_ref, lab_ref, waout_ref, lbg_ref, lbb_ref, wbout_ref,
                  are_ref, aim_ref, bblk_ref, cblk_ref, ssmd_ref, wcglu_ref, wout_ref,
                  x1_ref, nhist_ref, cv_ref, nsre_ref, nsim_ref,
                  hist_s, vprev_s, sst_s, kron_s):
    nb, tt, npos = cfg
    rows = nb * tt
    hb = HIST_A * nb
    i = pl.program_id(0)

    @pl.when(i == 0)
    def _init():
        hist_s[...] = _load_rows(hist_ref, nb, HIST_A)
        sst_s[0] = s0re_ref[...]
        sst_s[1] = s0im_ref[...]
        vprev_s[...] = jnp.zeros(vprev_s.shape, vprev_s.dtype)
        _build_kron(ws_ref, kron_s, nb, tt, npos)

    pos = i % 2
    prev_mix = ([_dot(vprev_s[g].astype(BF16), kron_s[1, g]) for g in range(N_GROUPS_B)]
                if npos > 1 else None)

    x = _load_rows(x_ref, nb, tt)
    h = _rms(x, gmix_ref[...]).astype(BF16)

    za = _dot(h, win_ref[:, 0:O_B])
    zc = _dot(h, win_ref[:, O_C:O_G])
    bus = [_dot(zc[:, q * SUB_IN:(q + 1) * SUB_IN].astype(BF16), bblk_ref[q])
           for q in range(SSM_SUB)]
    zb = _dot(h, win_ref[:, O_B:O_C])

    def gate(j):
        return jax.nn.sigmoid(_dot(h, win_ref[:, O_G + j * D_MODEL:O_G + (j + 1) * D_MODEL]))

    u = zb[:, :D_B]
    v = _ln(zb[:, D_B:], lbg_ref[...], lbb_ref[...])
    cv_ref[...] = v
    vm_parts = []
    for g in range(N_GROUPS_B):
        v_t = v[:, g * GW_B:(g + 1) * GW_B].T
        mixed_t = _dot(v_t.astype(BF16), kron_s[0 if npos == 1 else 2 * pos, g])
        if npos > 1:
            mixed_t = mixed_t + prev_mix[g]
            vprev_s[g] = jnp.where(pos == 0, v_t, 0.0)
        vm_parts.append(mixed_t.T)
    bias_rows = [jnp.broadcast_to(bsb_ref[0, t:t + 1, :], (nb, D_B)) for t in range(tt)]
    bias = jnp.concatenate(bias_rows, axis=0) if tt > 1 else bias_rows[0]
    vm = jnp.concatenate(vm_parts, axis=1) + bias
    uvm = (u * vm).astype(BF16)
    g_a = gate(0)

    a = za[:, :D_A] * jax.nn.sigmoid(za[:, D_A:])
    cat = jnp.concatenate([hist_s[...], a], axis=0)
    new_hist = cat[rows:rows + hb]
    hist_s[...] = new_hist
    nhist_ref[...] = new_hist
    cab = cab_ref[...]
    conv_blocks = []
    rb = min(CONV_ROWS, rows)
    for r0 in range(0, rows, rb):
        acc = jnp.broadcast_to(cab, (rb, D_A))
        for k in range(CONV_A_WIDTH):
            acc = acc + caw_ref[k:k + 1, :] * cat[k * nb + r0:k * nb + r0 + rb]
        conv_blocks.append(acc)
    a_conv = jnp.concatenate(conv_blocks, axis=0) if len(conv_blocks) > 1 else conv_blocks[0]
    a_act = _ln(a_conv, lag_ref[...], lab_ref[...])
    a_act = (a_act * jax.nn.sigmoid(a_act)).astype(BF16)
    m = g_a * _dot(a_act, waout_ref[...])
    m = m + gate(1) * _dot(uvm, wbout_ref[...])

    y_parts = []
    for q in range(SSM_SUB):
        uq = zc[:, q * SUB_IN:(q + 1) * SUB_IN]
        bu = bus[q]
        st = slice(q * SUB_ST, (q + 1) * SUB_ST)
        ar = jnp.broadcast_to(are_ref[:, st], (nb, SUB_ST))
        ai = jnp.broadcast_to(aim_ref[:, st], (nb, SUB_ST))
        sr = sst_s[0, :, st]
        si = sst_s[1, :, st]
        s_rows = []
        for t in range(tt):
            bt = bu[t * nb:(t + 1) * nb]
            nr = ar * sr - ai * si + bt[:, :SUB_ST]
            ni = ar * si + ai * sr + bt[:, SUB_ST:]
            sr, si = nr, ni
            s_rows.append(jnp.concatenate([nr, ni], axis=1))
        sst_s[0, :, st] = sr
        sst_s[1, :, st] = si
        nsre_ref[:, st] = sr
        nsim_ref[:, st] = si
        s_all = jnp.concatenate(s_rows, axis=0) if tt > 1 else s_rows[0]
        yq = _dot(s_all.astype(BF16), cblk_ref[q]) + ssmd_ref[:, q * SUB_IN:(q + 1) * SUB_IN] * uq
        y_parts.append(jax.nn.gelu(yq).astype(BF16))
    yc = jnp.concatenate(y_parts, axis=1)
    g_c = gate(2)
    gc = _dot(yc, wcglu_ref[...])
    m = m + g_c * (gc[:, :D_MODEL] * jax.nn.sigmoid(gc[:, D_MODEL:]))

    x1_ref[...] = x + _dot(m.astype(BF16), wout_ref[...])


def _ffn_kernel(cfg, x_ref, p_ref, hist_ref, gfin_ref, gffn_ref, wup_ref, cfw_ref, cfb_ref,
                wdown_ref, gple_ref, wpg_ref, wpe_ref,
                xo_ref, nhist_ref,
                hist_s):
    nb, tt, final = cfg
    rows = nb * tt
    hb = HIST_F * nb
    i = pl.program_id(0)

    @pl.when(i == 0)
    def _init():
        hist_s[...] = _load_rows(hist_ref, nb, HIST_F)

    pe = _dot(_load_rows(p_ref, nb, tt).astype(BF16), wpe_ref[...])
    x = x_ref[...]
    h = _rms(x, gffn_ref[...]).astype(BF16)
    f = None
    for c0, cw in FF_CHUNKS:
        halves = []
        for off in (c0, D_FF + c0):
            up = _dot(h, wup_ref[:, off:off + cw])
            cat = jnp.concatenate([hist_s[:, off:off + cw], up], axis=0)
            new_hist = cat[rows:rows + hb]
            hist_s[:, off:off + cw] = new_hist
            nhist_ref[:, off:off + cw] = new_hist
            y = cfb_ref[:, off:off + cw] + cfw_ref[0:1, off:off + cw] * cat[0:rows]
            y = y + cfw_ref[1:2, off:off + cw] * cat[nb:nb + rows]
            y = y + cfw_ref[2:3, off:off + cw] * up
            halves.append(y)
        act = (jax.nn.gelu(halves[0]) * halves[1]).astype(BF16)
        t = _dot(act, wdown_ref[c0:c0 + cw, :])
        f = t if f is None else f + t
    x2 = x + f
    h3 = _rms(x2, gple_ref[...]).astype(BF16)
    gate = jax.nn.sigmoid(_dot(h3, wpg_ref[...]))
    x3 = x2 + gate * pe
    if final:
        x3 = _rms(x3, gfin_ref[...])
    if len(xo_ref.shape) == 3:
        x3 = jnp.swapaxes(x3.reshape(tt, nb, D_MODEL), 0, 1)
    xo_ref[...] = x3


def _const_spec(shape):
    nd = len(shape)
    return pl.BlockSpec(shape, lambda i: (0,) * nd, pipeline_mode=pl.Buffered(1))


def _layer_spec(shape, layer):
    nd = len(shape)
    return pl.BlockSpec((None,) + tuple(shape[1:]), lambda i: (layer,) + (0,) * (nd - 1),
                        pipeline_mode=pl.Buffered(1))


def _pos_spec(shape, npos):
    nd = len(shape)
    return pl.BlockSpec((1,) + tuple(shape[1:]), lambda i: (i % npos,) + (0,) * (nd - 1))


def _out_const_spec(shape):
    nd = len(shape)
    return pl.BlockSpec(shape, lambda i: (0,) * nd)


MIXER_PARAMS = ('w_s', 'g_mix', 'w_in', 'conv_a_w', 'conv_a_b', 'ln_a_g', 'ln_a_b', 'w_a_out', 'ln_b_g',
                'ln_b_b', 'w_b_out', 'a_re', 'a_im', 'bblk', 'cblk', 'ssm_d', 'w_c_glu', 'w_out')
FFN_PARAMS = ('g_ffn', 'w_up', 'conv_f_w', 'conv_f_b', 'w_down', 'g_ple', 'w_pg', 'w_pe')


def _mixer_call(layer, nb, tt, nt, x, hist0, s0re, s0im, bsb, w):
    rows = nb * tt
    t_total = tt * nt
    npos = bsb.shape[0]
    last_start = ((t_total - 1) // CHUNK) * CHUNK
    n_last = (t_total - last_start) // tt
    stacked = [w[k] for k in MIXER_PARAMS]
    x_spec = (pl.BlockSpec((nb, tt, D_MODEL), lambda i: (0, i, 0)) if x.ndim == 3
              else pl.BlockSpec((rows, D_MODEL), lambda i: (i, 0)))
    in_specs = [x_spec,
                _layer_spec(hist0.shape, layer), _layer_spec(s0re.shape, layer),
                _layer_spec(s0im.shape, layer),
                _pos_spec(bsb.shape, npos)]
    in_specs += [_layer_spec(c.shape, layer) for c in stacked]
    out_shape = [
        jax.ShapeDtypeStruct((nt * rows, D_MODEL), F32),
        jax.ShapeDtypeStruct((HIST_A * nb, D_A), F32),
        jax.ShapeDtypeStruct((n_last * rows, D_B), F32),
        jax.ShapeDtypeStruct((nb, N_STATE), F32),
        jax.ShapeDtypeStruct((nb, N_STATE), F32),
    ]
    out_specs = [
        pl.BlockSpec((rows, D_MODEL), lambda i: (i, 0)),
        _out_const_spec((HIST_A * nb, D_A)),
        pl.BlockSpec((rows, D_B), lambda i: (jnp.maximum(i - (nt - n_last), 0), 0)),
        _out_const_spec((nb, N_STATE)),
        _out_const_spec((nb, N_STATE)),
    ]
    scratch = [
        pltpu.VMEM((HIST_A * nb, D_A), F32),
        pltpu.VMEM((N_GROUPS_B, GW_B, rows), F32),
        pltpu.VMEM((2, nb, N_STATE), F32),
        pltpu.VMEM((1 if npos == 1 else 3, N_GROUPS_B, rows, rows), BF16),
    ]
    return pl.pallas_call(
        functools.partial(_mixer_kernel, (nb, tt, npos)),
        grid=(nt,),
        in_specs=in_specs,
        out_specs=out_specs,
        out_shape=out_shape,
        scratch_shapes=scratch,
        compiler_params=pltpu.CompilerParams(
            dimension_semantics=("arbitrary",), vmem_limit_bytes=VMEM_LIMIT),
        name=f"mixer_b{nb}_t{tt}",
    )(x, hist0, s0re, s0im, bsb, *stacked)


def _ffn_call(layer, nb, tt, nt, final, out_bt, x, p, hist0, g_final, w):
    rows = nb * tt
    stacked = [w[k] for k in FFN_PARAMS]
    p_spec = (pl.BlockSpec((None, nb, tt, D_PLE), lambda i: (layer, 0, i, 0)) if p.ndim == 4
              else pl.BlockSpec((None, rows, D_PLE), lambda i: (layer, i, 0)))
    in_specs = [pl.BlockSpec((rows, D_MODEL), lambda i: (i, 0)), p_spec,
                _layer_spec(hist0.shape, layer), _const_spec(g_final.shape)]
    in_specs += [_layer_spec(c.shape, layer) for c in stacked]
    if out_bt:
        x_out = jax.ShapeDtypeStruct((nb, nt * tt, D_MODEL), F32)
        x_out_spec = pl.BlockSpec((nb, tt, D_MODEL), lambda i: (0, i, 0))
    else:
        x_out = jax.ShapeDtypeStruct((nt * rows, D_MODEL), F32)
        x_out_spec = pl.BlockSpec((rows, D_MODEL), lambda i: (i, 0))
    out_shape = [x_out, jax.ShapeDtypeStruct((HIST_F * nb, 2 * D_FF), F32)]
    out_specs = [x_out_spec, _out_const_spec((HIST_F * nb, 2 * D_FF))]
    return pl.pallas_call(
        functools.partial(_ffn_kernel, (nb, tt, final)),
        grid=(nt,),
        in_specs=in_specs,
        out_specs=out_specs,
        out_shape=out_shape,
        scratch_shapes=[pltpu.VMEM((HIST_F * nb, 2 * D_FF), F32)],
        compiler_params=pltpu.CompilerParams(
            dimension_semantics=("arbitrary",), vmem_limit_bytes=VMEM_LIMIT),
        name=f"ffn_b{nb}_t{tt}",
    )(x, p, hist0, g_final, *stacked)


def _mix_bias_rows(b_s, tt, npos):
    bsb = []
    for p in range(npos):
        bias = b_s[:, p * tt:(p + 1) * tt]
        bias = jnp.broadcast_to(bias.T[:, :, None], (tt, N_GROUPS_B, GW_B))
        bsb.append(bias.reshape(tt, D_B))
    return jnp.stack(bsb).astype(F32)


def _ssm_prep_kernel(are_ref, aim_ref, ldt_ref, bre_ref, bim_ref, cre_ref, cim_ref,
                     abr_ref, abi_ref, bblk_ref, cblk_ref):
    a_re, a_im = are_ref[...], aim_ref[...]
    dt = jnp.exp(ldt_ref[...])
    mag = jnp.exp(a_re * dt)
    ang = a_im * dt
    abr, abi = mag * jnp.cos(ang), mag * jnp.sin(ang)
    abr_ref[...] = abr
    abi_ref[...] = abi
    nr = abr - 1.0
    re_big = jnp.abs(a_re) >= jnp.abs(a_im)
    ratio = jnp.where(re_big, a_im, a_re) / jnp.where(re_big, a_re, a_im)
    den = jnp.where(re_big, a_re + a_im * ratio, a_re * ratio + a_im)
    cr = jnp.where(re_big, nr + abi * ratio, nr * ratio + abi) / den
    ci = jnp.where(re_big, abi - nr * ratio, abi * ratio - nr) / den
    for q in range(SSM_SUB):
        st = slice(q * SUB_ST, (q + 1) * SUB_ST)
        bre, bim = bre_ref[q], bim_ref[q]
        bblk_ref[q, :, 0:SUB_ST] = (cr[:, st] * bre - ci[:, st] * bim).astype(BF16)
        bblk_ref[q, :, SUB_ST:2 * SUB_ST] = (cr[:, st] * bim + ci[:, st] * bre).astype(BF16)
        cblk_ref[q, 0:SUB_ST, :] = cre_ref[q].astype(BF16)
        cblk_ref[q, SUB_ST:2 * SUB_ST, :] = (-cim_ref[q]).astype(BF16)


def _ssm_weights(a_re, a_im, log_dt, b_re, b_im, c_re, c_im):
    depth = a_re.shape[0]
    gl = N_GROUPS_C // SSM_SUB
    eye = jnp.eye(gl, dtype=F32)

    def pack_b(part):
        part = part.astype(F32).reshape(depth, SSM_SUB, gl, SSM_STATE, SSM_GROUP)
        part = jnp.swapaxes(part, 3, 4)[:, :, :, :, None, :] * eye[None, None, :, None, :, None]
        return part.reshape(depth, SSM_SUB, SUB_IN, SUB_ST)

    def pack_c(part):
        part = part.astype(F32).reshape(depth, SSM_SUB, gl, SSM_GROUP, SSM_STATE)
        part = jnp.swapaxes(part, 3, 4)[:, :, :, :, None, :] * eye[None, None, :, None, :, None]
        return part.reshape(depth, SSM_SUB, SUB_ST, SUB_IN)

    flat = lambda v: v.astype(F32).reshape(depth, 1, N_STATE)
    ldt = jnp.broadcast_to(log_dt.astype(F32)[:, :, None], (depth, N_GROUPS_C, SSM_STATE))
    ins = [flat(a_re), flat(a_im), flat(ldt), pack_b(b_re), pack_b(b_im), pack_c(c_re), pack_c(c_im)]
    spec = lambda shape: pl.BlockSpec((None,) + tuple(shape[1:]),
                                      lambda l: (l,) + (0,) * (len(shape) - 1))
    out_shape = [jax.ShapeDtypeStruct((depth, 1, N_STATE), F32),
                 jax.ShapeDtypeStruct((depth, 1, N_STATE), F32),
                 jax.ShapeDtypeStruct((depth, SSM_SUB, SUB_IN, 2 * SUB_ST), BF16),
                 jax.ShapeDtypeStruct((depth, SSM_SUB, 2 * SUB_ST, SUB_IN), BF16)]
    return pl.pallas_call(
        _ssm_prep_kernel,
        grid=(depth,),
        in_specs=[spec(v.shape) for v in ins],
        out_specs=[spec(o.shape) for o in out_shape],
        out_shape=out_shape,
        name="ssm_prep",
    )(*ins)


def _to_tb(x):
    *lead, b, t, c = x.shape
    return jnp.swapaxes(x, -3, -2).reshape(*lead, t * b, c)


def _from_tb(x, b):
    *lead, tb, c = x.shape
    return jnp.swapaxes(x.reshape(*lead, tb // b, b, c), -3, -2)


def _trunk(x, p, buf_a, s_re, s_im, buf_f, w, b_s, g_final, tt, tt_f):
    nb, t_total, _ = x.shape
    nt = t_total // tt
    depth = p.shape[0]
    npos = min(max(CHUNK // tt, 1), nt)
    assert npos in (1, 2) and tt <= CHUNK and CHUNK % tt == 0
    if tt > 1:
        xr, pr = x, p
    else:
        xr, pr = x.reshape(nb, D_MODEL), p.reshape(depth, nb, D_PLE)
    hist_a = _to_tb(buf_a)
    hist_f = buf_f
    s_re = s_re.reshape(depth, nb, N_STATE)
    s_im = s_im.reshape(depth, nb, N_STATE)
    la, lv, lre, lim, lf = [], [], [], [], []
    for i in range(depth):
        bsb = _mix_bias_rows(b_s[i], tt, npos)
        xr, nh_a, cv, nre, nim = _mixer_call(i, nb, tt, nt, xr, hist_a, s_re, s_im,
                                             bsb, w)
        last = i == depth - 1
        xr, nh_f = _ffn_call(i, nb, tt_f, t_total // tt_f, last, last and tt > 1, xr, pr, hist_f,
                             g_final, w)
        la.append(nh_a)
        lv.append(cv)
        lre.append(nre)
        lim.append(nim)
        lf.append(nh_f)
    state = lambda parts: jnp.stack(parts).reshape(depth, nb, N_GROUPS_C, SSM_STATE)
    return (xr.reshape(nb, t_total, D_MODEL), _from_tb(jnp.stack(la), nb),
            _from_tb(jnp.stack(lv), nb),
            state(lre), state(lim), _from_tb(jnp.stack(lf), nb))


def kernel(x_prompt, x_sample, state_conv_a, state_ssm_re, state_ssm_im, state_conv_ffn,
           p_prompt, p_sample, g_mix, w_in, conv_a_w, conv_a_b, ln_a_g, ln_a_b, w_a_out,
           ln_b_g, ln_b_b, w_s, b_s, w_b_out, ssm_a_re, ssm_a_im, ssm_log_dt, ssm_b_re,
           ssm_b_im, ssm_c_re, ssm_c_im, ssm_d, w_c_glu, w_out, g_ffn, w_up, conv_f_w,
           conv_f_b, w_down, g_ple, w_pg, w_pe, g_final):
    depth = w_in.shape[0]
    row = lambda a: a.reshape(depth, 1, -1).astype(F32)
    a_re, a_im, bblk, cblk = _ssm_weights(ssm_a_re, ssm_a_im, ssm_log_dt, ssm_b_re, ssm_b_im,
                                          ssm_c_re, ssm_c_im)
    w = dict(
        w_s=w_s.astype(F32), g_mix=row(g_mix), w_in=w_in.astype(BF16), conv_a_w=conv_a_w.astype(F32),
        conv_a_b=row(conv_a_b), ln_a_g=row(ln_a_g), ln_a_b=row(ln_a_b),
        w_a_out=w_a_out.astype(BF16), ln_b_g=row(ln_b_g), ln_b_b=row(ln_b_b),
        w_b_out=w_b_out.astype(BF16), a_re=a_re, a_im=a_im, bblk=bblk, cblk=cblk,
        ssm_d=row(ssm_d), w_c_glu=w_c_glu.astype(BF16), w_out=w_out.astype(BF16),
        g_ffn=row(g_ffn), w_up=w_up.astype(BF16), conv_f_w=conv_f_w.astype(F32),
        conv_f_b=row(conv_f_b), w_down=w_down.astype(BF16), g_ple=row(g_ple),
        w_pg=w_pg.astype(BF16), w_pe=w_pe.astype(BF16))
    gfin = g_final.reshape(1, -1).astype(F32)

    bp = x_prompt.shape[0]
    zero_a = jnp.zeros((depth, bp, HIST_A, D_A), F32)
    zero_s = jnp.zeros((depth, bp, N_GROUPS_C, SSM_STATE), F32)
    zero_f = jnp.zeros((depth, bp, HIST_F, 2 * D_FF), F32)
    y_p, ca_p, cv_p, re_p, im_p, cf_p = _trunk(
        x_prompt, p_prompt, zero_a, zero_s, zero_s, zero_f, w, b_s, gfin, TT_MIXER, TT_FFN)
    y_s, ca_s, cv_s, re_s, im_s, cf_s = _trunk(
        x_sample, p_sample, state_conv_a, state_ssm_re, state_ssm_im, state_conv_ffn,
        w, b_s, gfin, 1, 1)
    return (y_p, y_s, ca_p, ca_s, cv_p, cv_s, re_p, im_p, re_s, im_s, cf_p, cf_s)
```

```python
import functools

import jax
import jax.numpy as jnp
from jax import lax
from jax.experimental import pallas as pl
from jax.experimental.pallas import tpu as pltpu

D_MODEL = 1024
D_PLE = 256
D_A = 512
D_B = 512
D_C = 512
CONV_A_WIDTH = 31
CHUNK = 128
N_GROUPS_B = 4
GW_B = D_B // N_GROUPS_B
SSM_GROUP = 16
N_GROUPS_C = D_C // SSM_GROUP
SSM_STATE = 64
D_FF = 2816
FFN_CONV_WIDTH = 3
EPS = 1e-6

N_STATE = N_GROUPS_C * SSM_STATE
SSM_SUB = 4
SUB_IN = D_C // SSM_SUB
SUB_ST = N_STATE // SSM_SUB
HIST_A = CONV_A_WIDTH - 1
HIST_F = FFN_CONV_WIDTH - 1
O_B = 2 * D_A
O_C = O_B + 2 * D_B
O_G = O_C + D_C
FF_CHUNKS = ((0, 1536), (1536, 1280))
CONV_ROWS = 32
VMEM_LIMIT = 58 * 1024 * 1024
TT_MIXER = 64
TT_FFN = 64

F32 = jnp.float32
BF16 = jnp.bfloat16


def _rms(x, g):
    ms = jnp.mean(x * x, axis=-1, keepdims=True)
    return x * lax.rsqrt(ms + EPS) * g


def _ln(x, g, b):
    mu = jnp.mean(x, axis=-1, keepdims=True)
    xc = x - mu
    var = jnp.mean(xc * xc, axis=-1, keepdims=True)
    return xc * lax.rsqrt(var + EPS) * g + b


def _dot(a, b):
    return jnp.dot(a, b, preferred_element_type=F32)


def _load_rows(ref, nb, tt):
    v = ref[...]
    if v.ndim == 3:
        v = jnp.swapaxes(v, 0, 1).reshape(tt * nb, v.shape[-1])
    return v


def _build_kron(ws_ref, kron_s, nb, tt, npos):
    rows = nb * tt
    shift = nb.bit_length() - 1
    assert nb == 1 << shift
    r_i = lax.broadcasted_iota(jnp.int32, (rows, rows), 0)
    c_i = lax.broadcasted_iota(jnp.int32, (rows, rows), 1)
    same_batch = (r_i & (nb - 1)) == (c_i & (nb - 1))
    blocks = [(0, 0)] if npos == 1 else [(0, 0), (1, 0), (1, 1)]
    if tt > 1:
        e_r = lax.shift_right_logical(lax.broadcasted_iota(jnp.int32, (rows, tt), 0), shift)
        expand = jnp.where(e_r == lax.broadcasted_iota(jnp.int32, (rows, tt), 1), 1.0, 0.0)
        e_c = lax.shift_right_logical(lax.broadcasted_iota(jnp.int32, (tt, rows), 1), shift)
        expand_t = jnp.where(e_c == lax.broadcasted_iota(jnp.int32, (tt, rows), 0), 1.0, 0.0)
        expand, expand_t = expand.astype(BF16), expand_t.astype(BF16)
        tril = (lax.broadcasted_iota(jnp.int32, (tt, tt), 1)
                <= lax.broadcasted_iota(jnp.int32, (tt, tt), 0))
    for slab, (p, q) in enumerate(blocks):
        for g in range(N_GROUPS_B):
            blk = ws_ref[g, p * tt:(p + 1) * tt, q * tt:(q + 1) * tt]
            if tt == 1:
                k = jnp.broadcast_to(blk, (rows, rows))
            else:
                if p == q:
                    blk = jnp.where(tril, blk, 0.0)
                k = _dot(_dot(expand, blk.T.astype(BF16)).astype(BF16), expand_t)
            kron_s[slab, g] = jnp.where(same_batch, k, 0.0).astype(BF16)


def _mixer_kernel(cfg, x_ref, hist_ref, s0re_ref, s0im_ref, bsb_ref, ws_ref, gmix_ref, win_ref,
                  caw_ref, cab_ref, lag_ref, lab_ref, waout_ref, lbg_ref, lbb_ref, wbout_ref,
                  are_ref, aim_ref, bblk_ref, cblk_ref, ssmd_ref, wcglu_ref, wout_ref,
                  x1_ref, nhist_ref, cv_ref, nsre_ref, nsim_ref,
                  hist_s, vprev_s, sst_s, kron_s):
    nb, tt, npos = cfg
    rows = nb * tt
    hb = HIST_A * nb
    i = pl.program_id(0)

    @pl.when(i == 0)
    def _init():
        hist_s[...] = _load_rows(hist_ref, nb, HIST_A)
        sst_s[0] = s0re_ref[...]
        sst_s[1] = s0im_ref[...]
        vprev_s[...] = jnp.zeros(vprev_s.shape, vprev_s.dtype)
        _build_kron(ws_ref, kron_s, nb, tt, npos)

    pos = i % 2
    prev_mix = ([_dot(vprev_s[g].astype(BF16), kron_s[1, g]) for g in range(N_GROUPS_B)]
                if npos > 1 else None)

    x = _load_rows(x_ref, nb, tt)
    h = _rms(x, gmix_ref[...]).astype(BF16)

    za = _dot(h, win_ref[:, 0:O_B])
    zc = _dot(h, win_ref[:, O_C:O_G])
    bus = [_dot(zc[:, q * SUB_IN:(q + 1) * SUB_IN].astype(BF16), bblk_ref[q])
           for q in range(SSM_SUB)]
    zb = _dot(h, win_ref[:, O_B:O_C])

    def gate(j):
        return jax.nn.sigmoid(_dot(h, win_ref[:, O_G + j * D_MODEL:O_G + (j + 1) * D_MODEL]))

    u = zb[:, :D_B]
    v = _ln(zb[:, D_B:], lbg_ref[...], lbb_ref[...])
    cv_ref[...] = v
    vm_parts = []
    for g in range(N_GROUPS_B):
        v_t = v[:, g * GW_B:(g + 1) * GW_B].T
        mixed_t = _dot(v_t.astype(BF16), kron_s[0 if npos == 1 else 2 * pos, g])
        if npos > 1:
            mixed_t = mixed_t + prev_mix[g]
            vprev_s[g] = jnp.where(pos == 0, v_t, 0.0)
        vm_parts.append(mixed_t.T)
    bias_rows = [jnp.broadcast_to(bsb_ref[0, t:t + 1, :], (nb, D_B)) for t in range(tt)]
    bias = jnp.concatenate(bias_rows, axis=0) if tt > 1 else bias_rows[0]
    vm = jnp.concatenate(vm_parts, axis=1) + bias
    uvm = (u * vm).astype(BF16)
    g_a = gate(0)

    a = za[:, :D_A] * jax.nn.sigmoid(za[:, D_A:])
    cat = jnp.concatenate([hist_s[...], a], axis=0)
    new_hist = cat[rows:rows + hb]
    hist_s[...] = new_hist
    nhist_ref[...] = new_hist
    cab = cab_ref[...]
    conv_blocks = []
    rb = min(CONV_ROWS, rows)
    for r0 in range(0, rows, rb):
        acc = jnp.broadcast_to(cab, (rb, D_A))
        for k in range(CONV_A_WIDTH):
            acc = acc + caw_ref[k:k + 1, :] * cat[k * nb + r0:k * nb + r0 + rb]
        conv_blocks.append(acc)
    a_conv = jnp.concatenate(conv_blocks, axis=0) if len(conv_blocks) > 1 else conv_blocks[0]
    a_act = _ln(a_conv, lag_ref[...], lab_ref[...])
    a_act = (a_act * jax.nn.sigmoid(a_act)).astype(BF16)
    m = g_a * _dot(a_act, waout_ref[...])
    m = m + gate(1) * _dot(uvm, wbout_ref[...])

    y_parts = []
    for q in range(SSM_SUB):
        uq = zc[:, q * SUB_IN:(q + 1) * SUB_IN]
        bu = bus[q]
        st = slice(q * SUB_ST, (q + 1) * SUB_ST)
        ar = jnp.broadcast_to(are_ref[:, st], (nb, SUB_ST))
        ai = jnp.broadcast_to(aim_ref[:, st], (nb, SUB_ST))
        sr = sst_s[0, :, st]
        si = sst_s[1, :, st]
        s_rows = []
        for t in range(tt):
            bt = bu[t * nb:(t + 1) * nb]
            nr = ar * sr - ai * si + bt[:, :SUB_ST]
            ni = ar * si + ai * sr + bt[:, SUB_ST:]
            sr, si = nr, ni
            s_rows.append(jnp.concatenate([nr, ni], axis=1))
        sst_s[0, :, st] = sr
        sst_s[1, :, st] = si
        nsre_ref[:, st] = sr
        nsim_ref[:, st] = si
        s_all = jnp.concatenate(s_rows, axis=0) if tt > 1 else s_rows[0]
        yq = _dot(s_all.astype(BF16), cblk_ref[q]) + ssmd_ref[:, q * SUB_IN:(q + 1) * SUB_IN] * uq
        y_parts.append(jax.nn.gelu(yq).astype(BF16))
    yc = jnp.concatenate(y_parts, axis=1)
    g_c = gate(2)
    gc = _dot(yc, wcglu_ref[...])
    m = m + g_c * (gc[:, :D_MODEL] * jax.nn.sigmoid(gc[:, D_MODEL:]))

    x1_ref[...] = x + _dot(m.astype(BF16), wout_ref[...])


def _ffn_kernel(cfg, x_ref, p_ref, hist_ref, gfin_ref, gffn_ref, wup_ref, cfw_ref, cfb_ref,
                wdown_ref, gple_ref, wpg_ref, wpe_ref,
                xo_ref, nhist_ref,
                hist_s):
    nb, tt, final = cfg
    rows = nb * tt
    hb = HIST_F * nb
    i = pl.program_id(0)

    @pl.when(i == 0)
    def _init():
        hist_s[...] = _load_rows(hist_ref, nb, HIST_F)

    pe = _dot(_load_rows(p_ref, nb, tt).astype(BF16), wpe_ref[...])
    x = x_ref[...]
    h = _rms(x, gffn_ref[...]).astype(BF16)
    f = None
    for c0, cw in FF_CHUNKS:
        halves = []
        for off in (c0, D_FF + c0):
            up = _dot(h, wup_ref[:, off:off + cw])
            cat = jnp.concatenate([hist_s[:, off:off + cw], up], axis=0)
            new_hist = cat[rows:rows + hb]
            hist_s[:, off:off + cw] = new_hist
            nhist_ref[:, off:off + cw] = new_hist
            y = cfb_ref[:, off:off + cw] + cfw_ref[0:1, off:off + cw] * cat[0:rows]
            y = y + cfw_ref[1:2, off:off + cw] * cat[nb:nb + rows]
            y = y + cfw_ref[2:3, off:off + cw] * up
            halves.append(y)
        half = rows // 2 if rows % 16 == 0 else rows
        act = jnp.concatenate(
            [(jax.nn.gelu(halves[0][r0:r0 + half]) * halves[1][r0:r0 + half]).astype(BF16)
             for r0 in range(0, rows, half)], axis=0)
        t = _dot(act, wdown_ref[c0:c0 + cw, :])
        f = t if f is None else f + t
    x2 = x + f
    h3 = _rms(x2, gple_ref[...]).astype(BF16)
    gate = jax.nn.sigmoid(_dot(h3, wpg_ref[...]))
    x3 = x2 + gate * pe
    if final:
        x3 = _rms(x3, gfin_ref[...])
    if len(xo_ref.shape) == 3:
        x3 = jnp.swapaxes(x3.reshape(tt, nb, D_MODEL), 0, 1)
    xo_ref[...] = x3


def _const_spec(shape):
    nd = len(shape)
    return pl.BlockSpec(shape, lambda i: (0,) * nd, pipeline_mode=pl.Buffered(1))


def _layer_spec(shape, layer):
    nd = len(shape)
    return pl.BlockSpec((None,) + tuple(shape[1:]), lambda i: (layer,) + (0,) * (nd - 1),
                        pipeline_mode=pl.Buffered(1))


def _pos_spec(shape, npos):
    nd = len(shape)
    return pl.BlockSpec((1,) + tuple(shape[1:]), lambda i: (i % npos,) + (0,) * (nd - 1))


def _out_const_spec(shape):
    nd = len(shape)
    return pl.BlockSpec(shape, lambda i: (0,) * nd)


MIXER_PARAMS = ('w_s', 'g_mix', 'w_in', 'conv_a_w', 'conv_a_b', 'ln_a_g', 'ln_a_b', 'w_a_out', 'ln_b_g',
                'ln_b_b', 'w_b_out', 'a_re', 'a_im', 'bblk', 'cblk', 'ssm_d', 'w_c_glu', 'w_out')
FFN_PARAMS = ('g_ffn', 'w_up', 'conv_f_w', 'conv_f_b', 'w_down', 'g_ple', 'w_pg', 'w_pe')


def _mixer_call(layer, nb, tt, nt, x, hist0, s0re, s0im, bsb, w):
    rows = nb * tt
    t_total = tt * nt
    npos = bsb.shape[0]
    last_start = ((t_total - 1) // CHUNK) * CHUNK
    n_last = (t_total - last_start) // tt
    stacked = [w[k] for k in MIXER_PARAMS]
    x_spec = (pl.BlockSpec((nb, tt, D_MODEL), lambda i: (0, i, 0)) if x.ndim == 3
              else pl.BlockSpec((rows, D_MODEL), lambda i: (i, 0)))
    in_specs = [x_spec,
                _layer_spec(hist0.shape, layer), _layer_spec(s0re.shape, layer),
                _layer_spec(s0im.shape, layer),
                _pos_spec(bsb.shape, npos)]
    in_specs += [_layer_spec(c.shape, layer) for c in stacked]
    out_shape = [
        jax.ShapeDtypeStruct((nt * rows, D_MODEL), F32),
        jax.ShapeDtypeStruct((HIST_A * nb, D_A), F32),
        jax.ShapeDtypeStruct((n_last * rows, D_B), F32),
        jax.ShapeDtypeStruct((nb, N_STATE), F32),
        jax.ShapeDtypeStruct((nb, N_STATE), F32),
    ]
    out_specs = [
        pl.BlockSpec((rows, D_MODEL), lambda i: (i, 0)),
        _out_const_spec((HIST_A * nb, D_A)),
        pl.BlockSpec((rows, D_B), lambda i: (jnp.maximum(i - (nt - n_last), 0), 0)),
        _out_const_spec((nb, N_STATE)),
        _out_const_spec((nb, N_STATE)),
    ]
    scratch = [
        pltpu.VMEM((HIST_A * nb, D_A), F32),
        pltpu.VMEM((N_GROUPS_B, GW_B, rows), F32),
        pltpu.VMEM((2, nb, N_STATE), F32),
        pltpu.VMEM((1 if npos == 1 else 3, N_GROUPS_B, rows, rows), BF16),
    ]
    return pl.pallas_call(
        functools.partial(_mixer_kernel, (nb, tt, npos)),
        grid=(nt,),
        in_specs=in_specs,
        out_specs=out_specs,
        out_shape=out_shape,
        scratch_shapes=scratch,
        compiler_params=pltpu.CompilerParams(
            dimension_semantics=("arbitrary",), vmem_limit_bytes=VMEM_LIMIT),
        name=f"mixer_b{nb}_t{tt}",
    )(x, hist0, s0re, s0im, bsb, *stacked)


def _ffn_call(layer, nb, tt, nt, final, out_bt, x, p, hist0, g_final, w):
    rows = nb * tt
    stacked = [w[k] for k in FFN_PARAMS]
    p_spec = (pl.BlockSpec((None, nb, tt, D_PLE), lambda i: (layer, 0, i, 0)) if p.ndim == 4
              else pl.BlockSpec((None, rows, D_PLE), lambda i: (layer, i, 0)))
    in_specs = [pl.BlockSpec((rows, D_MODEL), lambda i: (i, 0)), p_spec,
                _layer_spec(hist0.shape, layer), _const_spec(g_final.shape)]
    in_specs += [_layer_spec(c.shape, layer) for c in stacked]
    if out_bt:
        x_out = jax.ShapeDtypeStruct((nb, nt * tt, D_MODEL), F32)
        x_out_spec = pl.BlockSpec((nb, tt, D_MODEL), lambda i: (0, i, 0))
    else:
        x_out = jax.ShapeDtypeStruct((nt * rows, D_MODEL), F32)
        x_out_spec = pl.BlockSpec((rows, D_MODEL), lambda i: (i, 0))
    out_shape = [x_out, jax.ShapeDtypeStruct((HIST_F * nb, 2 * D_FF), F32)]
    out_specs = [x_out_spec, _out_const_spec((HIST_F * nb, 2 * D_FF))]
    return pl.pallas_call(
        functools.partial(_ffn_kernel, (nb, tt, final)),
        grid=(nt,),
        in_specs=in_specs,
        out_specs=out_specs,
        out_shape=out_shape,
        scratch_shapes=[pltpu.VMEM((HIST_F * nb, 2 * D_FF), F32)],
        compiler_params=pltpu.CompilerParams(
            dimension_semantics=("arbitrary",), vmem_limit_bytes=VMEM_LIMIT),
        name=f"ffn_b{nb}_t{tt}",
    )(x, p, hist0, g_final, *stacked)


def _mix_bias_rows(b_s, tt, npos):
    bsb = []
    for p in range(npos):
        bias = b_s[:, p * tt:(p + 1) * tt]
        bias = jnp.broadcast_to(bias.T[:, :, None], (tt, N_GROUPS_B, GW_B))
        bsb.append(bias.reshape(tt, D_B))
    return jnp.stack(bsb).astype(F32)


def _ssm_prep_kernel(are_ref, aim_ref, ldt_ref, bre_ref, bim_ref, cre_ref, cim_ref,
                     abr_ref, abi_ref, bblk_ref, cblk_ref):
    a_re, a_im = are_ref[...], aim_ref[...]
    dt = jnp.exp(ldt_ref[...])
    mag = jnp.exp(a_re * dt)
    ang = a_im * dt
    abr, abi = mag * jnp.cos(ang), mag * jnp.sin(ang)
    abr_ref[...] = abr
    abi_ref[...] = abi
    nr = abr - 1.0
    re_big = jnp.abs(a_re) >= jnp.abs(a_im)
    ratio = jnp.where(re_big, a_im, a_re) / jnp.where(re_big, a_re, a_im)
    den = jnp.where(re_big, a_re + a_im * ratio, a_re * ratio + a_im)
    cr = jnp.where(re_big, nr + abi * ratio, nr * ratio + abi) / den
    ci = jnp.where(re_big, abi - nr * ratio, abi * ratio - nr) / den
    for q in range(SSM_SUB):
        st = slice(q * SUB_ST, (q + 1) * SUB_ST)
        bre, bim = bre_ref[q], bim_ref[q]
        bblk_ref[q, :, 0:SUB_ST] = (cr[:, st] * bre - ci[:, st] * bim).astype(BF16)
        bblk_ref[q, :, SUB_ST:2 * SUB_ST] = (cr[:, st] * bim + ci[:, st] * bre).astype(BF16)
        cblk_ref[q, 0:SUB_ST, :] = cre_ref[q].astype(BF16)
        cblk_ref[q, SUB_ST:2 * SUB_ST, :] = (-cim_ref[q]).astype(BF16)


def _ssm_weights(a_re, a_im, log_dt, b_re, b_im, c_re, c_im):
    depth = a_re.shape[0]
    gl = N_GROUPS_C // SSM_SUB
    eye = jnp.eye(gl, dtype=F32)

    def pack_b(part):
        part = part.astype(F32).reshape(depth, SSM_SUB, gl, SSM_STATE, SSM_GROUP)
        part = jnp.swapaxes(part, 3, 4)[:, :, :, :, None, :] * eye[None, None, :, None, :, None]
        return part.reshape(depth, SSM_SUB, SUB_IN, SUB_ST)

    def pack_c(part):
        part = part.astype(F32).reshape(depth, SSM_SUB, gl, SSM_GROUP, SSM_STATE)
        part = jnp.swapaxes(part, 3, 4)[:, :, :, :, None, :] * eye[None, None, :, None, :, None]
        return part.reshape(depth, SSM_SUB, SUB_ST, SUB_IN)

    flat = lambda v: v.astype(F32).reshape(depth, 1, N_STATE)
    ldt = jnp.broadcast_to(log_dt.astype(F32)[:, :, None], (depth, N_GROUPS_C, SSM_STATE))
    ins = [flat(a_re), flat(a_im), flat(ldt), pack_b(b_re), pack_b(b_im), pack_c(c_re), pack_c(c_im)]
    spec = lambda shape: pl.BlockSpec((None,) + tuple(shape[1:]),
                                      lambda l: (l,) + (0,) * (len(shape) - 1))
    out_shape = [jax.ShapeDtypeStruct((depth, 1, N_STATE), F32),
                 jax.ShapeDtypeStruct((depth, 1, N_STATE), F32),
                 jax.ShapeDtypeStruct((depth, SSM_SUB, SUB_IN, 2 * SUB_ST), BF16),
                 jax.ShapeDtypeStruct((depth, SSM_SUB, 2 * SUB_ST, SUB_IN), BF16)]
    return pl.pallas_call(
        _ssm_prep_kernel,
        grid=(depth,),
        in_specs=[spec(v.shape) for v in ins],
        out_specs=[spec(o.shape) for o in out_shape],
        out_shape=out_shape,
        name="ssm_prep",
    )(*ins)


def _to_tb(x):
    *lead, b, t, c = x.shape
    return jnp.swapaxes(x, -3, -2).reshape(*lead, t * b, c)


def _from_tb(x, b):
    *lead, tb, c = x.shape
    return jnp.swapaxes(x.reshape(*lead, tb // b, b, c), -3, -2)


def _trunk(x, p, buf_a, s_re, s_im, buf_f, w, b_s, g_final, tt, tt_f):
    nb, t_total, _ = x.shape
    nt = t_total // tt
    depth = p.shape[0]
    npos = min(max(CHUNK // tt, 1), nt)
    assert npos in (1, 2) and tt <= CHUNK and CHUNK % tt == 0
    if tt > 1:
        xr, pr = x, p
    else:
        xr, pr = x.reshape(nb, D_MODEL), p.reshape(depth, nb, D_PLE)
    hist_a = _to_tb(buf_a)
    hist_f = buf_f
    s_re = s_re.reshape(depth, nb, N_STATE)
    s_im = s_im.reshape(depth, nb, N_STATE)
    la, lv, lre, lim, lf = [], [], [], [], []
    for i in range(depth):
        bsb = _mix_bias_rows(b_s[i], tt, npos)
        xr, nh_a, cv, nre, nim = _mixer_call(i, nb, tt, nt, xr, hist_a, s_re, s_im,
                                             bsb, w)
        last = i == depth - 1
        xr, nh_f = _ffn_call(i, nb, tt_f, t_total // tt_f, last, last and tt > 1, xr, pr, hist_f,
                             g_final, w)
        la.append(nh_a)
        lv.append(cv)
        lre.append(nre)
        lim.append(nim)
        lf.append(nh_f)
    state = lambda parts: jnp.stack(parts).reshape(depth, nb, N_GROUPS_C, SSM_STATE)
    return (xr.reshape(nb, t_total, D_MODEL), _from_tb(jnp.stack(la), nb),
            _from_tb(jnp.stack(lv), nb),
            state(lre), state(lim), _from_tb(jnp.stack(lf), nb))


def kernel(x_prompt, x_sample, state_conv_a, state_ssm_re, state_ssm_im, state_conv_ffn,
           p_prompt, p_sample, g_mix, w_in, conv_a_w, conv_a_b, ln_a_g, ln_a_b, w_a_out,
           ln_b_g, ln_b_b, w_s, b_s, w_b_out, ssm_a_re, ssm_a_im, ssm_log_dt, ssm_b_re,
           ssm_b_im, ssm_c_re, ssm_c_im, ssm_d, w_c_glu, w_out, g_ffn, w_up, conv_f_w,
           conv_f_b, w_down, g_ple, w_pg, w_pe, g_final):
    depth = w_in.shape[0]
    row = lambda a: a.reshape(depth, 1, -1).astype(F32)
    a_re, a_im, bblk, cblk = _ssm_weights(ssm_a_re, ssm_a_im, ssm_log_dt, ssm_b_re, ssm_b_im,
                                          ssm_c_re, ssm_c_im)
    w = dict(
        w_s=w_s.astype(F32), g_mix=row(g_mix), w_in=w_in.astype(BF16), conv_a_w=conv_a_w.astype(F32),
        conv_a_b=row(conv_a_b), ln_a_g=row(ln_a_g), ln_a_b=row(ln_a_b),
        w_a_out=w_a_out.astype(BF16), ln_b_g=row(ln_b_g), ln_b_b=row(ln_b_b),
        w_b_out=w_b_out.astype(BF16), a_re=a_re, a_im=a_im, bblk=bblk, cblk=cblk,
        ssm_d=row(ssm_d), w_c_glu=w_c_glu.astype(BF16), w_out=w_out.astype(BF16),
        g_ffn=row(g_ffn), w_up=w_up.astype(BF16), conv_f_w=conv_f_w.astype(F32),
        conv_f_b=row(conv_f_b), w_down=w_down.astype(BF16), g_ple=row(g_ple),
        w_pg=w_pg.astype(BF16), w_pe=w_pe.astype(BF16))
    gfin = g_final.reshape(1, -1).astype(F32)

    bp = x_prompt.shape[0]
    zero_a = jnp.zeros((depth, bp, HIST_A, D_A), F32)
    zero_s = jnp.zeros((depth, bp, N_GROUPS_C, SSM_STATE), F32)
    zero_f = jnp.zeros((depth, bp, HIST_F, 2 * D_FF), F32)
    y_p, ca_p, cv_p, re_p, im_p, cf_p = _trunk(
        x_prompt, p_prompt, zero_a, zero_s, zero_s, zero_f, w, b_s, gfin, TT_MIXER, TT_FFN)
    y_s, ca_s, cv_s, re_s, im_s, cf_s = _trunk(
        x_sample, p_sample, state_conv_a, state_ssm_re, state_ssm_im, state_conv_ffn,
        w, b_s, gfin, 1, 1)
    return (y_p, y_s, ca_p, ca_s, cv_p, cv_s, re_p, im_p, re_s, im_s, cf_p, cf_s)
```
